```python
import jax, jax.numpy as jnp
from jax import lax
import numpy as np

D_MODEL = 2048
BATCH = 4
SEQ = 4096
DEPTH = 1
DEC_BATCH = 4
DEC_SEQ = 2048
PAST_LEN = 128

MIX_WIDTH = D_MODEL
FOURIER_WIDTH = MIX_WIDTH // 2
CONV_WIDTH = MIX_WIDTH - FOURIER_WIDTH
FOURIER_GROUPS = 4
FOURIER_GROUP_DIM = FOURIER_WIDTH // FOURIER_GROUPS
CONV_GROUPS = 4
CONV_KERNEL = 31
CONV_PAD = CONV_KERNEL // 2
D_FF = 4 * D_MODEL
PLE_DIM = 256
ALPHA = (2.0 * DEPTH) ** 0.25
BETA = (8.0 * DEPTH) ** -0.25
LN_EPS = 1e-5

kernel_name = "fnet_conformer_hybrid_encoder"


def layer_norm(x, g, b):
    xf = x.astype(jnp.float32)
    mu = jnp.mean(xf, axis=-1, keepdims=True)
    var = jnp.mean(jnp.square(xf - mu), axis=-1, keepdims=True)
    y = (xf - mu) * lax.rsqrt(var + LN_EPS)
    return (y * g.astype(jnp.float32) + b.astype(jnp.float32)).astype(x.dtype)


def fourier_mix(u):
    B, S, _ = u.shape
    uh = u.reshape(B, S, FOURIER_GROUPS, FOURIER_GROUP_DIM).astype(jnp.float32)
    f = jnp.fft.fft2(uh, axes=(1, 3), norm="ortho")
    return jnp.real(f).reshape(B, S, FOURIER_WIDTH).astype(u.dtype)


def conformer_conv(val, gate, w_dw, b_dw, g, b):
    h = val * jax.nn.sigmoid(gate)
    h = lax.conv_general_dilated(
        h, w_dw[:, None, :].astype(h.dtype),
        window_strides=(1,), padding=[(CONV_PAD, CONV_PAD)],
        dimension_numbers=("NWC", "WIO", "NWC"),
        feature_group_count=CONV_WIDTH) + b_dw
    h = layer_norm(h, g, b)
    return jax.nn.silu(h)


def encoder_layer(x, p, w_in, w_dw, b_dw, conv_ln_g, conv_ln_b, w_out, ln1_g, ln1_b,
                  w_ff1, b_ff1, w_ff2, b_ff2, ln2_g, ln2_b, w_gate, b_gate, w_ple,
                  ln3_g, ln3_b):
    u = jnp.einsum("bsd,dc->bsc", x, w_in)
    u_f = u[..., :FOURIER_WIDTH]
    u_v = u[..., FOURIER_WIDTH:FOURIER_WIDTH + CONV_WIDTH]
    u_g = u[..., FOURIER_WIDTH + CONV_WIDTH:]
    heads = jnp.concatenate(
        [fourier_mix(u_f), conformer_conv(u_v, u_g, w_dw, b_dw, conv_ln_g, conv_ln_b)], axis=-1)
    mix = jnp.einsum("bsc,cd->bsd", heads, w_out)
    x = layer_norm(ALPHA * x + mix, ln1_g, ln1_b)
    hid = jnp.square(jax.nn.relu(jnp.einsum("bsd,df->bsf", x, w_ff1) + b_ff1))
    ff = jnp.einsum("bsf,fd->bsd", hid, w_ff2) + b_ff2
    x = layer_norm(ALPHA * x + ff, ln2_g, ln2_b)
    gate = jax.nn.sigmoid(jnp.einsum("bsd,de->bse", x, w_gate) + b_gate)
    e = gate * jnp.einsum("bsk,kd->bsd", p, w_ple)
    return layer_norm(x + e, ln3_g, ln3_b)


def run_trunk(x, p, emb_ln_g, emb_ln_b, w_in, w_dw, b_dw, conv_ln_g, conv_ln_b, w_out,
              ln1_g, ln1_b, w_ff1, b_ff1, w_ff2, b_ff2, ln2_g, ln2_b, w_gate, b_gate,
              w_ple, ln3_g, ln3_b):
    x = layer_norm(x, emb_ln_g, emb_ln_b)
    for i in range(DEPTH):
        x = encoder_layer(x, p[i], w_in[i], w_dw[i], b_dw[i], conv_ln_g[i], conv_ln_b[i],
                          w_out[i], ln1_g[i], ln1_b[i], w_ff1[i], b_ff1[i], w_ff2[i],
                          b_ff2[i], ln2_g[i], ln2_b[i], w_gate[i], b_gate[i], w_ple[i],
                          ln3_g[i], ln3_b[i])
    return x


def setup_inputs(seed: int = 0) -> dict:
    key = jax.random.key(seed)
    ks = jax.random.split(key, 32)
    f32 = jnp.float32
    n = lambda k, shape, s: (jax.random.normal(k, shape, f32) * s)
    gain = lambda k, shape: 1.0 + 0.05 * jax.random.normal(k, shape, f32)
    bias = lambda k, shape: 0.02 * jax.random.normal(k, shape, f32)
    L = DEPTH
    return {
        "x_prompt": jax.random.normal(ks[0], (BATCH, SEQ, D_MODEL), f32),
        "x_sample": jax.random.normal(ks[1], (DEC_BATCH, DEC_SEQ, D_MODEL), f32),
        "p_prompt": jax.random.normal(ks[2], (DEPTH, BATCH, SEQ, PLE_DIM), f32),
        "p_sample": jax.random.normal(ks[3], (DEPTH, DEC_BATCH, DEC_SEQ, PLE_DIM), f32),
        "emb_ln_g": gain(ks[4], (D_MODEL,)),
        "emb_ln_b": bias(ks[5], (D_MODEL,)),
        "w_in": n(ks[6], (L, D_MODEL, FOURIER_WIDTH + 2 * CONV_WIDTH), D_MODEL ** -0.5),
        "w_dw": n(ks[7], (L, CONV_KERNEL, CONV_WIDTH), CONV_KERNEL ** -0.5),
        "b_dw": bias(ks[8], (L, CONV_WIDTH)),
        "conv_ln_g": gain(ks[9], (L, CONV_WIDTH)),
        "conv_ln_b": bias(ks[10], (L, CONV_WIDTH)),
        "w_out": n(ks[11], (L, MIX_WIDTH, D_MODEL), BETA * MIX_WIDTH ** -0.5),
        "ln1_g": gain(ks[12], (L, D_MODEL)),
        "ln1_b": bias(ks[13], (L, D_MODEL)),
        "w_ff1": n(ks[14], (L, D_MODEL, D_FF), D_MODEL ** -0.5),
        "b_ff1": bias(ks[15], (L, D_FF)),
        "w_ff2": n(ks[16], (L, D_FF, D_MODEL), BETA * D_FF ** -0.5),
        "b_ff2": bias(ks[17], (L, D_MODEL)),
        "ln2_g": gain(ks[18], (L, D_MODEL)),
        "ln2_b": bias(ks[19], (L, D_MODEL)),
        "w_gate": n(ks[20], (L, D_MODEL, D_MODEL), D_MODEL ** -0.5),
        "b_gate": bias(ks[21], (L, D_MODEL)),
        "w_ple": n(ks[22], (L, PLE_DIM, D_MODEL), BETA * PLE_DIM ** -0.5),
        "ln3_g": gain(ks[23], (L, D_MODEL)),
        "ln3_b": bias(ks[24], (L, D_MODEL)),
    }


def reference(x_prompt, x_sample, p_prompt, p_sample, emb_ln_g, emb_ln_b, w_in, w_dw, b_dw,
              conv_ln_g, conv_ln_b, w_out, ln1_g, ln1_b, w_ff1, b_ff1, w_ff2, b_ff2,
              ln2_g, ln2_b, w_gate, b_gate, w_ple, ln3_g, ln3_b):
    y_prompt = run_trunk(x_prompt, p_prompt, emb_ln_g, emb_ln_b, w_in, w_dw, b_dw, conv_ln_g,
                         conv_ln_b, w_out, ln1_g, ln1_b, w_ff1, b_ff1, w_ff2, b_ff2, ln2_g,
                         ln2_b, w_gate, b_gate, w_ple, ln3_g, ln3_b)
    y_sample = run_trunk(x_sample, p_sample, emb_ln_g, emb_ln_b, w_in, w_dw, b_dw, conv_ln_g,
                         conv_ln_b, w_out, ln1_g, ln1_b, w_ff1, b_ff1, w_ff2, b_ff2, ln2_g,
                         ln2_b, w_gate, b_gate, w_ple, ln3_g, ln3_b)
    return (y_prompt, y_sample)
```

```python
import functools

import numpy as np
import jax
import jax.numpy as jnp
from jax import lax
from jax.experimental import pallas as pl
from jax.experimental.pallas import tpu as pltpu

D_MODEL = 2048
FOURIER_WIDTH = 1024
CONV_WIDTH = 1024
FOURIER_GROUPS = 4
FOURIER_GROUP_DIM = 256
CONV_KERNEL = 31
CONV_PAD = 15
D_FF = 8192
PLE_DIM = 256
DEPTH = 1
ALPHA = (2.0 * DEPTH) ** 0.25
LN_EPS = 1e-5

LANES = 128
HALO = 16
MIB = 1024 * 1024

F32 = jnp.float32
BF16 = jnp.bfloat16


def _ln(x, g, b):
    mu = jnp.mean(x, axis=-1, keepdims=True)
    xc = x - mu
    var = jnp.mean(xc * xc, axis=-1, keepdims=True)
    return xc * lax.rsqrt(var + LN_EPS) * g + b


def _dot(a, b):
    return jnp.dot(a, b, preferred_element_type=F32)


def _params(sem, vmem_mib):
    return pltpu.CompilerParams(dimension_semantics=sem, vmem_limit_bytes=vmem_mib * MIB)


def _const_spec(shape):
    nd = len(shape)
    return pl.BlockSpec(shape, lambda *_: (0,) * nd, pipeline_mode=pl.Buffered(1))


def _cos_sin(n):
    k = np.arange(n, dtype=np.int64)
    ang = 2.0 * np.pi * ((k[:, None] * k[None, :]) % n).astype(np.float64) / n
    return np.cos(ang), np.sin(ang)


def _channel_dft_matrix():
    c, s = _cos_sin(FOURIER_GROUP_DIM)
    scale = 1.0 / np.sqrt(FOURIER_GROUP_DIM)
    return np.concatenate([c, s], axis=1) * scale


def _seq_factors(seq):
    n2 = 64
    n1 = seq // n2
    return n1, n2


def _seq_dft_constants(seq):
    n1, n2 = _seq_factors(seq)
    c1, s1 = _cos_sin(n1)
    sc1 = 1.0 / np.sqrt(n1)
    fa = np.concatenate([c1, -s1], axis=0) * sc1
    fb = np.concatenate([-s1, -c1], axis=0) * sc1
    k1 = np.arange(n1, dtype=np.int64)[:, None]
    m2 = np.arange(n2, dtype=np.int64)[None, :]
    ang = 2.0 * np.pi * ((k1 * m2) % seq).astype(np.float64) / seq
    tw_c = np.cos(ang)
    tw_s = np.sin(ang)
    c2, s2 = _cos_sin(n2)
    sc2 = 1.0 / np.sqrt(n2)
    return fa, fb, tw_c, tw_s, c2 * sc2, s2 * sc2


def _k1_body(x_ref, eg_ref, eb_ref, win_ref, cs_ref, a_ref, b_ref, h_ref):
    xn = _ln(x_ref[...], eg_ref[...], eb_ref[...]).astype(BF16)
    uf = _dot(xn, win_ref[:, 0:FOURIER_WIDTH])
    gd = FOURIER_GROUP_DIM
    for g in range(FOURIER_GROUPS):
        ab = _dot(uf[:, g * gd:(g + 1) * gd].astype(BF16), cs_ref[...])
        a_ref[:, g * gd:(g + 1) * gd] = ab[:, :gd].astype(BF16)
        b_ref[:, g * gd:(g + 1) * gd] = ab[:, gd:].astype(BF16)
    uv = _dot(xn, win_ref[:, FOURIER_WIDTH:FOURIER_WIDTH + CONV_WIDTH])
    ug = _dot(xn, win_ref[:, FOURIER_WIDTH + CONV_WIDTH:])
    h_ref[...] = uv * jax.nn.sigmoid(ug)


def _k1(x2d, eg, eb, win, cs, tm=512):
    t = x2d.shape[0]
    row = lambda i: (i, 0)
    return pl.pallas_call(
        _k1_body,
        grid=(t // tm,),
        in_specs=[
            pl.BlockSpec((tm, D_MODEL), row),
            _const_spec((1, D_MODEL)),
            _const_spec((1, D_MODEL)),
            _const_spec(win.shape),
            _const_spec(cs.shape),
        ],
        out_specs=[
            pl.BlockSpec((tm, FOURIER_WIDTH), row),
            pl.BlockSpec((tm, FOURIER_WIDTH), row),
            pl.BlockSpec((tm, CONV_WIDTH), row),
        ],
        out_shape=[
            jax.ShapeDtypeStruct((t, FOURIER_WIDTH), BF16),
            jax.ShapeDtypeStruct((t, FOURIER_WIDTH), BF16),
            jax.ShapeDtypeStruct((t, CONV_WIDTH), F32),
        ],
        compiler_params=_params(("parallel",), 52),
        name="k1_in_proj",
    )(x2d, eg, eb, win, cs)


def _k2a_body(a_ref, b_ref, fa_ref, fb_ref, tc_ref, ts_ref, yr_ref, yi_ref, *, n1, group):
    y = _dot(fa_ref[...], a_ref[0]) + _dot(fb_ref[...], b_ref[0])
    yr = y[:n1]
    yi = y[n1:]
    per = FOURIER_WIDTH // LANES
    for g in range(group):
        c = tc_ref[:, g * LANES:(g + 1) * LANES]
        s = ts_ref[:, g * LANES:(g + 1) * LANES]
        for l in range(per):
            lo = g * FOURIER_WIDTH + l * LANES
            r = yr[:, lo:lo + LANES]
            i = yi[:, lo:lo + LANES]
            yr_ref[0, :, lo:lo + LANES] = (r * c + i * s).astype(BF16)
            yi_ref[0, :, lo:lo + LANES] = (i * c - r * s).astype(BF16)


def _k2a(a3, b3, fa, fb, tw_c, tw_s, n1, n2, group=8):
    nb = a3.shape[0]
    tn = group * FOURIER_WIDTH
    blk = lambda b, j: (b, 0, j)
    tw = lambda b, j: (0, j)
    return pl.pallas_call(
        functools.partial(_k2a_body, n1=n1, group=group),
        grid=(nb, n2 // group),
        in_specs=[
            pl.BlockSpec((1, n1, tn), blk),
            pl.BlockSpec((1, n1, tn), blk),
            _const_spec(fa.shape),
            _const_spec(fb.shape),
            pl.BlockSpec((n1, group * LANES), tw),
            pl.BlockSpec((n1, group * LANES), tw),
        ],
        out_specs=[pl.BlockSpec((1, n1, tn), blk), pl.BlockSpec((1, n1, tn), blk)],
        out_shape=[jax.ShapeDtypeStruct(a3.shape, BF16), jax.ShapeDtypeStruct(a3.shape, BF16)],
        compiler_params=_params(("parallel", "parallel"), 40),
        name="k2a_seq_dft_major",
    )(a3, b3, fa, fb, tw_c, tw_s)


def _k2b_body(yr_ref, yi_ref, fc_ref, fs_ref, o_ref, *, group):
    for g in range(group):
        x = _dot(fc_ref[...], yr_ref[0, g]) + _dot(fs_ref[...], yi_ref[0, g])
        o_ref[0, :, g * FOURIER_WIDTH:(g + 1) * FOURIER_WIDTH] = x.astype(BF16)


def _k2b(yr4, yi4, fc, fs, group=8):
    nb, n1, n2, cw = yr4.shape
    in_blk = lambda b, j: (b, j, 0, 0)
    out_blk = lambda b, j: (b, 0, j)
    return pl.pallas_call(
        functools.partial(_k2b_body, group=group),
        grid=(nb, n1 // group),
        in_specs=[
            pl.BlockSpec((1, group, n2, cw), in_blk),
            pl.BlockSpec((1, group, n2, cw), in_blk),
            _const_spec(fc.shape),
            _const_spec(fs.shape),
        ],
        out_specs=pl.BlockSpec((1, n2, group * cw), out_blk),
        out_shape=jax.ShapeDtypeStruct((nb, n2, n1 * cw), BF16),
        compiler_params=_params(("parallel", "parallel"), 40),
        name="k2b_seq_dft_minor",
    )(yr4, yi4, fc, fs)


CONV_ROWS = 64
CONV_LANES = 128


def _k3_body(x_ref, fm_ref, h_ref, hp_ref, hn_ref, eg_ref, eb_ref, wdw_ref, bdw_ref,
             cg_ref, cb_ref, wout_ref, g1_ref, b1_ref, o_ref, hbuf, cbuf, heads, *, tm):
    i = pl.program_id(1)
    last = pl.num_programs(1) - 1
    hbuf[0:HALO, :] = jnp.where(i > 0, hp_ref[0], 0.0)
    hbuf[HALO:HALO + tm, :] = h_ref[0]
    hbuf[HALO + tm:, :] = jnp.where(i < last, hn_ref[0], 0.0)

    base = HALO - CONV_PAD
    win_rows = CONV_ROWS + 2 * HALO

    def chunk(c, carry):
        r0 = pl.multiple_of(c * CONV_ROWS, CONV_ROWS)
        for l0 in range(0, CONV_WIDTH, CONV_LANES):
            lanes = slice(l0, l0 + CONV_LANES)
            win = hbuf[pl.ds(r0, win_rows), lanes]
            acc = jnp.zeros((CONV_ROWS, CONV_LANES), F32)
            for r in range(8):
                shifted = win if r == 0 else pltpu.roll(win, win_rows - r, axis=0)
                for q in range(win_rows // 8):
                    j = 8 * q + r - base
                    if 0 <= j < CONV_KERNEL:
                        acc = acc + shifted[8 * q:8 * q + CONV_ROWS, :] * wdw_ref[j:j + 1, lanes]
            cbuf[pl.ds(r0, CONV_ROWS), lanes] = acc + bdw_ref[:, lanes]
        return carry

    lax.fori_loop(0, tm // CONV_ROWS, chunk, 0)

    cn = _ln(cbuf[...], cg_ref[...], cb_ref[...])
    heads[:, 0:FOURIER_WIDTH] = fm_ref[0]
    heads[:, FOURIER_WIDTH:] = (cn * jax.nn.sigmoid(cn)).astype(BF16)

    mix = _dot(heads[...], wout_ref[...])
    xn = _ln(x_ref[0], eg_ref[...], eb_ref[...])
    o_ref[0] = _ln(ALPHA * xn + mix, g1_ref[...], b1_ref[...])


def _k3(x, fm, h, eg, eb, wdw, bdw, cg, cb, wout, g1, b1, tm=512):
    nb, seq, _ = x.shape
    nh = tm // HALO
    blk = lambda b, i: (b, i, 0)
    prev = lambda b, i: (b, jnp.maximum(i * nh - 1, 0), 0)
    nxt = lambda b, i: (b, jnp.minimum((i + 1) * nh, seq // HALO - 1), 0)
    return pl.pallas_call(
        functools.partial(_k3_body, tm=tm),
        grid=(nb, seq // tm),
        in_specs=[
            pl.BlockSpec((1, tm, D_MODEL), blk),
            pl.BlockSpec((1, tm, FOURIER_WIDTH), blk),
            pl.BlockSpec((1, tm, CONV_WIDTH), blk),
            pl.BlockSpec((1, HALO, CONV_WIDTH), prev),
            pl.BlockSpec((1, HALO, CONV_WIDTH), nxt),
            _const_spec((1, D_MODEL)),
            _const_spec((1, D_MODEL)),
            _const_spec(wdw.shape),
            _const_spec((1, CONV_WIDTH)),
            _const_spec((1, CONV_WIDTH)),
            _const_spec((1, CONV_WIDTH)),
            _const_spec(wout.shape),
            _const_spec((1, D_MODEL)),
            _const_spec((1, D_MODEL)),
        ],
        out_specs=pl.BlockSpec((1, tm, D_MODEL), blk),
        out_shape=jax.ShapeDtypeStruct(x.shape, F32),
        scratch_shapes=[
            pltpu.VMEM((tm + 2 * HALO, CONV_WIDTH), F32),
            pltpu.VMEM((tm, CONV_WIDTH), F32),
            pltpu.VMEM((tm, D_MODEL), BF16),
        ],
        compiler_params=_params(("parallel", "parallel"), 52),
        name="k3_mix",
    )(x, fm, h, h, h, eg, eb, wdw, bdw, cg, cb, wout, g1, b1)


LN_ROWS = 256


def _k4_body(x1_ref, w1_ref, bf1_ref, w2_ref, bf2_ref, g2_ref, b2_ref, o_ref, xb, *, tm):
    f = pl.program_id(1)

    @pl.when(f == 0)
    def _():
        xb[...] = x1_ref[...].astype(BF16)
        o_ref[...] = jnp.zeros_like(o_ref)

    hid = jnp.maximum(_dot(xb[...], w1_ref[...]) + bf1_ref[...], 0.0)
    o_ref[...] += _dot((hid * hid).astype(BF16), w2_ref[...])

    @pl.when(f == pl.num_programs(1) - 1)
    def _():
        for r0 in range(0, tm, LN_ROWS):
            rows = slice(r0, r0 + LN_ROWS)
            y = ALPHA * x1_ref[rows, :] + (o_ref[rows, :] + bf2_ref[...])
            o_ref[rows, :] = _ln(y, g2_ref[...], b2_ref[...])


def _k4(x1, w1, bf1, w2, bf2, g2, b2, tm=1024, tf=512):
    t = x1.shape[0]
    return pl.pallas_call(
        functools.partial(_k4_body, tm=tm),
        grid=(t // tm, D_FF // tf),
        in_specs=[
            pl.BlockSpec((tm, D_MODEL), lambda i, f: (i, 0)),
            pl.BlockSpec((D_MODEL, tf), lambda i, f: (0, f)),
            pl.BlockSpec((1, tf), lambda i, f: (0, f)),
            pl.BlockSpec((tf, D_MODEL), lambda i, f: (f, 0)),
            _const_spec((1, D_MODEL)),
            _const_spec((1, D_MODEL)),
            _const_spec((1, D_MODEL)),
        ],
        out_specs=pl.BlockSpec((tm, D_MODEL), lambda i, f: (i, 0)),
        out_shape=jax.ShapeDtypeStruct(x1.shape, F32),
        scratch_shapes=[pltpu.VMEM((tm, D_MODEL), BF16)],
        compiler_params=_params(("parallel", "arbitrary"), 56),
        name="k4_ffn",
    )(x1, w1, bf1, w2, bf2, g2, b2)


def _k5_body(x2_ref, p_ref, wg_ref, bg_ref, wp_ref, g3_ref, b3_ref, o_ref):
    x2 = x2_ref[...]
    gate = jax.nn.sigmoid(_dot(x2.astype(BF16), wg_ref[...]) + bg_ref[...])
    e = gate * _dot(p_ref[...].astype(BF16), wp_ref[...])
    o_ref[...] = _ln(x2 + e, g3_ref[...], b3_ref[...])


def _k5(x2, p2d, wg, bg, wp, g3, b3, tm=512):
    t = x2.shape[0]
    row = lambda i: (i, 0)
    return pl.pallas_call(
        _k5_body,
        grid=(t // tm,),
        in_specs=[
            pl.BlockSpec((tm, D_MODEL), row),
            pl.BlockSpec((tm, PLE_DIM), row),
            _const_spec(wg.shape),
            _const_spec((1, D_MODEL)),
            _const_spec(wp.shape),
            _const_spec((1, D_MODEL)),
            _const_spec((1, D_MODEL)),
        ],
        out_specs=pl.BlockSpec((tm, D_MODEL), row),
        out_shape=jax.ShapeDtypeStruct(x2.shape, F32),
        compiler_params=_params(("parallel",), 48),
        name="k5_ple",
    )(x2, p2d, wg, bg, wp, g3, b3)


def _trunk(x, p, wts, cs):
    nb, seq, _ = x.shape
    t = nb * seq
    n1, n2 = _seq_factors(seq)
    fa, fb, tw_c, tw_s, fc, fs = _seq_dft_constants(seq)
    fa, fb, fc, fs = (jnp.asarray(m, BF16) for m in (fa, fb, fc, fs))
    tw_c = jnp.repeat(jnp.asarray(tw_c, F32), LANES, axis=1)
    tw_s = jnp.repeat(jnp.asarray(tw_s, F32), LANES, axis=1)

    a, b, h = _k1(x.reshape(t, D_MODEL), wts["emb_g"], wts["emb_b"], wts["w_in"], cs)

    cw = FOURIER_WIDTH
    yr, yi = _k2a(a.reshape(nb, n1, n2 * cw), b.reshape(nb, n1, n2 * cw),
                  fa, fb, tw_c, tw_s, n1, n2)
    fm = _k2b(yr.reshape(nb, n1, n2, cw), yi.reshape(nb, n1, n2, cw), fc, fs)
    fm = fm.reshape(nb, seq, cw)

    x1 = _k3(x, fm, h.reshape(nb, seq, CONV_WIDTH), wts["emb_g"], wts["emb_b"],
             wts["w_dw"], wts["b_dw"], wts["conv_g"], wts["conv_b"], wts["w_out"],
             wts["ln1_g"], wts["ln1_b"])
    x2 = _k4(x1.reshape(t, D_MODEL), wts["w_ff1"], wts["b_ff1"], wts["w_ff2"], wts["b_ff2"],
             wts["ln2_g"], wts["ln2_b"])
    y = _k5(x2, p.reshape(t, PLE_DIM), wts["w_gate"], wts["b_gate"], wts["w_ple"],
            wts["ln3_g"], wts["ln3_b"])
    return y.reshape(nb, seq, D_MODEL)


def kernel(x_prompt, x_sample, p_prompt, p_sample, emb_ln_g, emb_ln_b, w_in, w_dw, b_dw,
           conv_ln_g, conv_ln_b, w_out, ln1_g, ln1_b, w_ff1, b_ff1, w_ff2, b_ff2,
           ln2_g, ln2_b, w_gate, b_gate, w_ple, ln3_g, ln3_b):
    row = lambda v: v.reshape(1, -1).astype(F32)
    wts = {
        "emb_g": row(emb_ln_g), "emb_b": row(emb_ln_b),
        "w_in": w_in[0].astype(BF16),
        "w_dw": w_dw[0].astype(F32), "b_dw": row(b_dw[0]),
        "conv_g": row(conv_ln_g[0]), "conv_b": row(conv_ln_b[0]),
        "w_out": w_out[0].astype(BF16),
        "ln1_g": row(ln1_g[0]), "ln1_b": row(ln1_b[0]),
        "w_ff1": w_ff1[0].astype(BF16), "b_ff1": row(b_ff1[0]),
        "w_ff2": w_ff2[0].astype(BF16), "b_ff2": row(b_ff2[0]),
        "ln2_g": row(ln2_g[0]), "ln2_b": row(ln2_b[0]),
        "w_gate": w_gate[0].astype(BF16), "b_gate": row(b_gate[0]),
        "w_ple": w_ple[0].astype(BF16),
        "ln3_g": row(ln3_g[0]), "ln3_b": row(ln3_b[0]),
    }
    cs = jnp.asarray(_channel_dft_matrix(), BF16)
    y_prompt = _trunk(x_prompt, p_prompt[0], wts, cs)
    y_sample = _trunk(x_sample, p_sample[0], wts, cs)
    return (y_prompt, y_sample)
```

```python
import functools

import numpy as np
import jax
import jax.numpy as jnp
from jax import lax
from jax.experimental import pallas as pl
from jax.experimental.pallas import tpu as pltpu

D_MODEL = 2048
FOURIER_WIDTH = 1024
CONV_WIDTH = 1024
FOURIER_GROUPS = 4
FOURIER_GROUP_DIM = 256
CONV_KERNEL = 31
CONV_PAD = 15
D_FF = 8192
PLE_DIM = 256
DEPTH = 1
ALPHA = (2.0 * DEPTH) ** 0.25
LN_EPS = 1e-5

LANES = 128
HALO = 16
RADIX = 16
SLAB = 16
MIB = 1024 * 1024

F32 = jnp.float32
BF16 = jnp.bfloat16


def _ln(x, g, b):
    mu = jnp.mean(x, axis=-1, keepdims=True)
    xc = x - mu
    var = jnp.mean(xc * xc, axis=-1, keepdims=True)
    return xc * lax.rsqrt(var + LN_EPS) * g + b


def _dot(a, b):
    return jnp.dot(a, b, preferred_element_type=F32)


def _params(sem, vmem_mib, flags=None):
    return pltpu.CompilerParams(dimension_semantics=sem, vmem_limit_bytes=vmem_mib * MIB, flags=flags)


def _const_spec(shape):
    nd = len(shape)
    return pl.BlockSpec(shape, lambda *_: (0,) * nd, pipeline_mode=pl.Buffered(1))


def _mxu_const(a):
    return jnp.asarray(np.asarray(a, np.float32)).astype(BF16)


def _cos_sin(n):
    k = np.arange(n, dtype=np.int64)
    ang = 2.0 * np.pi * ((k[:, None] * k[None, :]) % n).astype(np.float64) / n
    return np.cos(ang), np.sin(ang)


def _channel_dft_matrix():
    c, s = _cos_sin(FOURIER_GROUP_DIM)
    scale = 1.0 / np.sqrt(FOURIER_GROUP_DIM)
    return np.concatenate([c, s], axis=1) * scale


def _radix_matrices():
    c, s = _cos_sin(RADIX)
    eye = np.eye(SLAB)
    sc = 1.0 / np.sqrt(RADIX)
    ka = np.concatenate([np.kron(c, eye), np.kron(-s, eye)], axis=0) * sc
    kb = np.concatenate([np.kron(-s, eye), np.kron(-c, eye)], axis=0) * sc
    return ka, kb


def _twiddles(seq):
    m = seq // RADIX
    k1 = np.arange(RADIX, dtype=np.int64)[None, :, None]
    n2 = (np.arange(m // SLAB, dtype=np.int64)[:, None, None] * SLAB
          + np.arange(SLAB, dtype=np.int64)[None, None, :])
    ang = 2.0 * np.pi * ((k1 * n2) % seq).astype(np.float64) / seq
    ang = ang.reshape(m // SLAB, RADIX * SLAB)
    return np.cos(ang), np.sin(ang)


def _minor_dft_matrices(seq):
    m = seq // RADIX
    c, s = _cos_sin(m)
    sc = 1.0 / np.sqrt(m)
    return c * sc, s * sc


def _k1_body(x_ref, eg_ref, eb_ref, win_ref, cs_ref, ka_ref, kb_ref, tc_ref, ts_ref,
             yr_ref, yi_ref, h_ref, axn_ref, a_scr, b_scr):
    rows = RADIX * SLAB
    x = x_ref[0].reshape(rows, D_MODEL)
    xn = _ln(x, eg_ref[...], eb_ref[...])
    axn_ref[0] = (ALPHA * xn).reshape(RADIX, SLAB, D_MODEL)
    xn = xn.astype(BF16)

    uv = _dot(xn, win_ref[:, FOURIER_WIDTH:FOURIER_WIDTH + CONV_WIDTH])
    ug = _dot(xn, win_ref[:, FOURIER_WIDTH + CONV_WIDTH:])
    h_ref[0] = (uv * jax.nn.sigmoid(ug)).reshape(RADIX, SLAB, CONV_WIDTH)

    uf = _dot(xn, win_ref[:, 0:FOURIER_WIDTH])
    gd = FOURIER_GROUP_DIM
    for g in range(FOURIER_GROUPS):
        ab = _dot(uf[:, g * gd:(g + 1) * gd].astype(BF16), cs_ref[...])
        a_scr[:, g * gd:(g + 1) * gd] = ab[:, :gd].astype(BF16)
        b_scr[:, g * gd:(g + 1) * gd] = ab[:, gd:].astype(BF16)

    y = _dot(ka_ref[...], a_scr[...]) + _dot(kb_ref[...], b_scr[...])
    yr = y[:rows]
    yi = y[rows:]
    c = tc_ref[0]
    s = ts_ref[0]
    for l in range(FOURIER_WIDTH // LANES):
        lanes = slice(l * LANES, (l + 1) * LANES)
        r = yr[:, lanes]
        i = yi[:, lanes]
        yr_ref[0, :, :, lanes] = (r * c + i * s).astype(BF16).reshape(RADIX, SLAB, LANES)
        yi_ref[0, :, :, lanes] = (i * c - r * s).astype(BF16).reshape(RADIX, SLAB, LANES)


def _k1(x4, eg, eb, win, cs, ka, kb, tw_c, tw_s):
    nb, _, m, _ = x4.shape
    blk = lambda b, j: (b, 0, j, 0)
    tw = lambda b, j: (j, 0, 0)
    rows = RADIX * SLAB
    return pl.pallas_call(
        _k1_body,
        grid=(nb, m // SLAB),
        in_specs=[
            pl.BlockSpec((1, RADIX, SLAB, D_MODEL), blk),
            _const_spec((1, D_MODEL)),
            _const_spec((1, D_MODEL)),
            _const_spec(win.shape),
            _const_spec(cs.shape),
            _const_spec(ka.shape),
            _const_spec(kb.shape),
            pl.BlockSpec((1, rows, LANES), tw),
            pl.BlockSpec((1, rows, LANES), tw),
        ],
        out_specs=[
            pl.BlockSpec((1, RADIX, SLAB, FOURIER_WIDTH), blk),
            pl.BlockSpec((1, RADIX, SLAB, FOURIER_WIDTH), blk),
            pl.BlockSpec((1, RADIX, SLAB, CONV_WIDTH), blk),
            pl.BlockSpec((1, RADIX, SLAB, D_MODEL), blk),
        ],
        out_shape=[
            jax.ShapeDtypeStruct((nb, RADIX, m, FOURIER_WIDTH), BF16),
            jax.ShapeDtypeStruct((nb, RADIX, m, FOURIER_WIDTH), BF16),
            jax.ShapeDtypeStruct((nb, RADIX, m, CONV_WIDTH), F32),
            jax.ShapeDtypeStruct((nb, RADIX, m, D_MODEL), F32),
        ],
        scratch_shapes=[
            pltpu.VMEM((rows, FOURIER_WIDTH), BF16),
            pltpu.VMEM((rows, FOURIER_WIDTH), BF16),
        ],
        compiler_params=_params(("parallel", "parallel"), 48),
        name="k1_in_proj",
    )(x4, eg, eb, win, cs, ka, kb, tw_c, tw_s)


def _k2_body(yr_ref, yi_ref, fc_ref, fs_ref, o_ref, *, group):
    for g in range(group):
        o_ref[0, g] = _dot(fc_ref[...], yr_ref[0, g]) + _dot(fs_ref[...], yi_ref[0, g])


def _k2(yr, yi, fc, fs, group=4):
    nb, _, m, cw = yr.shape
    blk = lambda b, j: (b, j, 0, 0)
    return pl.pallas_call(
        functools.partial(_k2_body, group=group),
        grid=(nb, RADIX // group),
        in_specs=[
            pl.BlockSpec((1, group, m, cw), blk),
            pl.BlockSpec((1, group, m, cw), blk),
            _const_spec(fc.shape),
            _const_spec(fs.shape),
        ],
        out_specs=pl.BlockSpec((1, group, m, cw), blk),
        out_shape=jax.ShapeDtypeStruct(yr.shape, F32),
        compiler_params=_params(("parallel", "parallel"), 40),
        name="k2_seq_dft_minor",
    )(yr, yi, fc, fs)


CONV_ROWS = 64


def _k3_body(axn_ref, fm_ref, h_ref, hp_ref, hn_ref, wdw_ref, bdw_ref,
             cg_ref, cb_ref, wout_ref, g1_ref, b1_ref, o_ref, hbuf, cbuf, fmt, heads, *, tm):
    i = pl.program_id(1)
    last = pl.num_programs(1) - 1
    for l in range(CONV_WIDTH // LANES):
        lanes = slice(l * LANES, (l + 1) * LANES)
        hbuf[l, 0:HALO, :] = jnp.where(i > 0, hp_ref[0, :, lanes], 0.0)
        hbuf[l, HALO:HALO + tm, :] = h_ref[0, :, lanes]
        hbuf[l, HALO + tm:, :] = jnp.where(i < last, hn_ref[0, :, lanes], 0.0)

    nl = FOURIER_WIDTH // LANES
    for k1 in range(RADIX):
        for l in range(nl):
            fmt[l, pl.ds(k1, tm // RADIX, stride=RADIX), :] = fm_ref[0, k1, :, l * LANES:(l + 1) * LANES]
    for l in range(nl):
        heads[:, l * LANES:(l + 1) * LANES] = fmt[l].astype(BF16)

    base = HALO - CONV_PAD

    def conv_lane_chunk(l, carry):
        lanes = pl.ds(pl.multiple_of(l * LANES, LANES), LANES)
        for r0 in range(0, tm, CONV_ROWS):
            acc = jnp.zeros((CONV_ROWS, LANES), F32)
            for j in range(CONV_KERNEL):
                acc = acc + hbuf[l, r0 + base + j:r0 + base + j + CONV_ROWS, :] * wdw_ref[j:j + 1, lanes]
            cbuf[r0:r0 + CONV_ROWS, lanes] = acc + bdw_ref[:, lanes]
        return carry

    lax.fori_loop(0, CONV_WIDTH // LANES, conv_lane_chunk, 0)

    cn = _ln(cbuf[...], cg_ref[...], cb_ref[...])
    heads[:, FOURIER_WIDTH:] = (cn * jax.nn.sigmoid(cn)).astype(BF16)
    mix = _dot(heads[...], wout_ref[...])
    o_ref[0] = _ln(axn_ref[0] + mix, g1_ref[...], b1_ref[...])


def _k3(axn, fm4, h, wdw, bdw, cg, cb, wout, g1, b1, tm=512):
    nb, seq, _ = axn.shape
    nh = tm // HALO
    blk = lambda b, i: (b, i, 0)
    fblk = lambda b, i: (b, 0, i, 0)
    prev = lambda b, i: (b, jnp.maximum(i * nh - 1, 0), 0)
    nxt = lambda b, i: (b, jnp.minimum((i + 1) * nh, seq // HALO - 1), 0)
    return pl.pallas_call(
        functools.partial(_k3_body, tm=tm),
        grid=(nb, seq // tm),
        in_specs=[
            pl.BlockSpec((1, tm, D_MODEL), blk),
            pl.BlockSpec((1, RADIX, tm // RADIX, FOURIER_WIDTH), fblk),
            pl.BlockSpec((1, tm, CONV_WIDTH), blk),
            pl.BlockSpec((1, HALO, CONV_WIDTH), prev),
            pl.BlockSpec((1, HALO, CONV_WIDTH), nxt),
            _const_spec(wdw.shape),
            _const_spec((1, CONV_WIDTH)),
            _const_spec((1, CONV_WIDTH)),
            _const_spec((1, CONV_WIDTH)),
            _const_spec(wout.shape),
            _const_spec((1, D_MODEL)),
            _const_spec((1, D_MODEL)),
        ],
        out_specs=pl.BlockSpec((1, tm, D_MODEL), blk),
        out_shape=jax.ShapeDtypeStruct(axn.shape, F32),
        scratch_shapes=[
            pltpu.VMEM((CONV_WIDTH // LANES, tm + 2 * HALO, LANES), F32),
            pltpu.VMEM((tm, CONV_WIDTH), F32),
            pltpu.VMEM((FOURIER_WIDTH // LANES, tm, LANES), F32),
            pltpu.VMEM((tm, D_MODEL), BF16),
        ],
        compiler_params=_params(("parallel", "parallel"), 52),
        name="k3_mix",
    )(axn, fm4, h, h, h, wdw, bdw, cg, cb, wout, g1, b1)


LN_ROWS = 256


def _k4_body(x1_ref, w1_ref, bf1_ref, w2_ref, bf2_ref, g2_ref, b2_ref, o_ref, xb, *, tm):
    f = pl.program_id(1)

    @pl.when(f == 0)
    def _():
        xb[...] = x1_ref[...].astype(BF16)
        o_ref[...] = jnp.zeros_like(o_ref)

    hid = jnp.maximum(_dot(xb[...], w1_ref[...]) + bf1_ref[...], 0.0)
    o_ref[...] += _dot((hid * hid).astype(BF16), w2_ref[...])

    @pl.when(f == pl.num_programs(1) - 1)
    def _():
        for r0 in range(0, tm, LN_ROWS):
            rows = slice(r0, r0 + LN_ROWS)
            y = ALPHA * x1_ref[rows, :] + (o_ref[rows, :] + bf2_ref[...])
            o_ref[rows, :] = _ln(y, g2_ref[...], b2_ref[...])


def _k4(x1, w1, bf1, w2, bf2, g2, b2, tm=1024, tf=512):
    t = x1.shape[0]
    return pl.pallas_call(
        functools.partial(_k4_body, tm=tm),
        grid=(t // tm, D_FF // tf),
        in_specs=[
            pl.BlockSpec((tm, D_MODEL), lambda i, f: (i, 0)),
            pl.BlockSpec((D_MODEL, tf), lambda i, f: (0, f)),
            pl.BlockSpec((1, tf), lambda i, f: (0, f)),
            pl.BlockSpec((tf, D_MODEL), lambda i, f: (f, 0)),
            _const_spec((1, D_MODEL)),
            _const_spec((1, D_MODEL)),
            _const_spec((1, D_MODEL)),
        ],
        out_specs=pl.BlockSpec((tm, D_MODEL), lambda i, f: (i, 0)),
        out_shape=jax.ShapeDtypeStruct(x1.shape, F32),
        scratch_shapes=[pltpu.VMEM((tm, D_MODEL), BF16)],
        compiler_params=_params(("parallel", "arbitrary"), 56),
        name="k4_ffn",
    )(x1, w1, bf1, w2, bf2, g2, b2)


K5_SPLIT = 2


def _k5_body(x2_ref, p_ref, wg_ref, bg_ref, wp_ref, g3_ref, b3_ref, o_ref):
    part = x2_ref.shape[0] // K5_SPLIT
    for s in range(K5_SPLIT):
        rows = slice(s * part, (s + 1) * part)
        x2 = x2_ref[rows, :]
        gate = jax.nn.sigmoid(_dot(x2.astype(BF16), wg_ref[...]) + bg_ref[...])
        e = gate * _dot(p_ref[rows, :].astype(BF16), wp_ref[...])
        o_ref[rows, :] = _ln(x2 + e, g3_ref[...], b3_ref[...])


def _k5(x2, p2d, wg, bg, wp, g3, b3, tm=512):
    t = x2.shape[0]
    row = lambda i: (i, 0)
    return pl.pallas_call(
        _k5_body,
        grid=(t // tm,),
        in_specs=[
            pl.BlockSpec((tm, D_MODEL), row),
            pl.BlockSpec((tm, PLE_DIM), row),
            _const_spec(wg.shape),
            _const_spec((1, D_MODEL)),
            _const_spec(wp.shape),
            _const_spec((1, D_MODEL)),
            _const_spec((1, D_MODEL)),
        ],
        out_specs=pl.BlockSpec((tm, D_MODEL), row),
        out_shape=jax.ShapeDtypeStruct(x2.shape, F32),
        compiler_params=_params(("parallel",), 48),
        name="k5_ple",
    )(x2, p2d, wg, bg, wp, g3, b3)


def _trunk(x, p, wts, consts):
    nb, seq, _ = x.shape
    t = nb * seq
    m = seq // RADIX
    cs, ka, kb = consts
    tw_c, tw_s = _twiddles(seq)
    lane_rows = lambda a: jnp.broadcast_to(jnp.asarray(a, F32)[:, :, None], a.shape + (LANES,))
    fc, fs = (_mxu_const(a) for a in _minor_dft_matrices(seq))

    yr, yi, h, axn = _k1(x.reshape(nb, RADIX, m, D_MODEL), wts["emb_g"], wts["emb_b"], wts["w_in"],
                    cs, ka, kb, lane_rows(tw_c), lane_rows(tw_s))
    fm4 = _k2(yr, yi, fc, fs)

    x1 = _k3(axn.reshape(nb, seq, D_MODEL), fm4, h.reshape(nb, seq, CONV_WIDTH),
             wts["w_dw"], wts["b_dw"], wts["conv_g"], wts["conv_b"], wts["w_out"],
             wts["ln1_g"], wts["ln1_b"])
    x2 = _k4(x1.reshape(t, D_MODEL), wts["w_ff1"], wts["b_ff1"], wts["w_ff2"], wts["b_ff2"],
             wts["ln2_g"], wts["ln2_b"])
    y = _k5(x2, p.reshape(t, PLE_DIM), wts["w_gate"], wts["b_gate"], wts["w_ple"],
            wts["ln3_g"], wts["ln3_b"])
    return y.reshape(nb, seq, D_MODEL)


def kernel(x_prompt, x_sample, p_prompt, p_sample, emb_ln_g, emb_ln_b, w_in, w_dw, b_dw,
           conv_ln_g, conv_ln_b, w_out, ln1_g, ln1_b, w_ff1, b_ff1, w_ff2, b_ff2,
           ln2_g, ln2_b, w_gate, b_gate, w_ple, ln3_g, ln3_b):
    row = lambda v: v.reshape(1, -1).astype(F32)
    wts = {
        "emb_g": row(emb_ln_g), "emb_b": row(emb_ln_b),
        "w_in": w_in[0].astype(BF16),
        "w_dw": w_dw[0].astype(F32), "b_dw": row(b_dw[0]),
        "conv_g": row(conv_ln_g[0]), "conv_b": row(conv_ln_b[0]),
        "w_out": w_out[0].astype(BF16),
        "ln1_g": row(ln1_g[0]), "ln1_b": row(ln1_b[0]),
        "w_ff1": w_ff1[0].astype(BF16), "b_ff1": row(b_ff1[0]),
        "w_ff2": w_ff2[0].astype(BF16), "b_ff2": row(b_ff2[0]),
        "ln2_g": row(ln2_g[0]), "ln2_b": row(ln2_b[0]),
        "w_gate": w_gate[0].astype(BF16), "b_gate": row(b_gate[0]),
        "w_ple": w_ple[0].astype(BF16),
        "ln3_g": row(ln3_g[0]), "ln3_b": row(ln3_b[0]),
    }
    ka, kb = _radix_matrices()
    consts = (_mxu_const(_channel_dft_matrix()), _mxu_const(ka), _mxu_const(kb))
    y_prompt = _trunk(x_prompt, p_prompt[0], wts, consts)
    y_sample = _trunk(x_sample, p_sample[0], wts, consts)
    return (y_prompt, y_sample)
```

```python
import functools

import numpy as np
import jax
import jax.numpy as jnp
from jax import lax
from jax.experimental import pallas as pl
from jax.experimental.pallas import tpu as pltpu

D_MODEL = 2048
FOURIER_WIDTH = 1024
CONV_WIDTH = 1024
FOURIER_GROUPS = 4
FOURIER_GROUP_DIM = 256
CONV_KERNEL = 31
CONV_PAD = 15
D_FF = 8192
PLE_DIM = 256
DEPTH = 1
ALPHA = (2.0 * DEPTH) ** 0.25
LN_EPS = 1e-5

LANES = 128
HALO = 16
RADIX = 16
SLAB = 16
MIB = 1024 * 1024

F32 = jnp.float32
BF16 = jnp.bfloat16


def _ln(x, g, b):
    mu = jnp.mean(x, axis=-1, keepdims=True)
    xc = x - mu
    var = jnp.mean(xc * xc, axis=-1, keepdims=True)
    return xc * lax.rsqrt(var + LN_EPS) * g + b


def _dot(a, b):
    return jnp.dot(a, b, preferred_element_type=F32)


def _params(sem, vmem_mib, flags=None):
    return pltpu.CompilerParams(dimension_semantics=sem, vmem_limit_bytes=vmem_mib * MIB, flags=flags)


def _const_spec(shape):
    nd = len(shape)
    return pl.BlockSpec(shape, lambda *_: (0,) * nd, pipeline_mode=pl.Buffered(1))


def _mxu_const(a):
    return jnp.asarray(np.asarray(a, np.float32)).astype(BF16)


def _cos_sin(n):
    k = np.arange(n, dtype=np.int64)
    ang = 2.0 * np.pi * ((k[:, None] * k[None, :]) % n).astype(np.float64) / n
    return np.cos(ang), np.sin(ang)


def _channel_dft_matrix():
    c, s = _cos_sin(FOURIER_GROUP_DIM)
    scale = 1.0 / np.sqrt(FOURIER_GROUP_DIM)
    return np.concatenate([c, s], axis=1) * scale


def _radix_matrices():
    c, s = _cos_sin(RADIX)
    eye = np.eye(SLAB)
    sc = 1.0 / np.sqrt(RADIX)
    ka = np.concatenate([np.kron(c, eye), np.kron(-s, eye)], axis=0) * sc
    kb = np.concatenate([np.kron(-s, eye), np.kron(-c, eye)], axis=0) * sc
    return ka, kb


def _minor_dft_matrices(seq):
    m = seq // RADIX
    c, s = _cos_sin(m)
    sc = 1.0 / np.sqrt(m)
    fc = jnp.asarray((c * sc).astype(np.float32))[None]
    fs = jnp.asarray((s * sc).astype(np.float32))[None]
    k1 = np.arange(RADIX, dtype=np.int64)[:, None]
    n2 = np.arange(m, dtype=np.int64)[None, :]
    ang = 2.0 * np.pi * ((k1 * n2) % seq).astype(np.float64) / seq
    tc = jnp.asarray(np.cos(ang).astype(np.float32))[:, None, :]
    ts = jnp.asarray(np.sin(ang).astype(np.float32))[:, None, :]
    return (fc * tc - fs * ts).astype(BF16), (fs * tc + fc * ts).astype(BF16)


def _k1_body(x_ref, eg_ref, eb_ref, win_ref, cs_ref, ka_ref, kb_ref,
             yr_ref, yi_ref, h_ref, axn_ref, a_scr, b_scr):
    rows = RADIX * SLAB
    x = x_ref[0].reshape(rows, D_MODEL)
    xn = _ln(x, eg_ref[...], eb_ref[...])
    axn_ref[0] = (ALPHA * xn).reshape(RADIX, SLAB, D_MODEL)
    xn = xn.astype(BF16)

    uv = _dot(xn, win_ref[:, FOURIER_WIDTH:FOURIER_WIDTH + CONV_WIDTH])
    ug = _dot(xn, win_ref[:, FOURIER_WIDTH + CONV_WIDTH:])
    h_ref[0] = (uv * jax.nn.sigmoid(ug)).reshape(RADIX, SLAB, CONV_WIDTH)

    uf = _dot(xn, win_ref[:, 0:FOURIER_WIDTH])
    gd = FOURIER_GROUP_DIM
    for g in range(FOURIER_GROUPS):
        ab = _dot(uf[:, g * gd:(g + 1) * gd].astype(BF16), cs_ref[...])
        a_scr[:, g * gd:(g + 1) * gd] = ab[:, :gd].astype(BF16)
        b_scr[:, g * gd:(g + 1) * gd] = ab[:, gd:].astype(BF16)

    y = _dot(ka_ref[...], a_scr[...]) + _dot(kb_ref[...], b_scr[...])
    yr_ref[0] = y[:rows].astype(BF16).reshape(RADIX, SLAB, FOURIER_WIDTH)
    yi_ref[0] = y[rows:].astype(BF16).reshape(RADIX, SLAB, FOURIER_WIDTH)


def _k1(x4, eg, eb, win, cs, ka, kb):
    nb, _, m, _ = x4.shape
    blk = lambda b, j: (b, 0, j, 0)
    rows = RADIX * SLAB
    return pl.pallas_call(
        _k1_body,
        grid=(nb, m // SLAB),
        in_specs=[
            pl.BlockSpec((1, RADIX, SLAB, D_MODEL), blk),
            _const_spec((1, D_MODEL)),
            _const_spec((1, D_MODEL)),
            _const_spec(win.shape),
            _const_spec(cs.shape),
            _const_spec(ka.shape),
            _const_spec(kb.shape),
        ],
        out_specs=[
            pl.BlockSpec((1, RADIX, SLAB, FOURIER_WIDTH), blk),
            pl.BlockSpec((1, RADIX, SLAB, FOURIER_WIDTH), blk),
            pl.BlockSpec((1, RADIX, SLAB, CONV_WIDTH), blk),
            pl.BlockSpec((1, RADIX, SLAB, D_MODEL), blk),
        ],
        out_shape=[
            jax.ShapeDtypeStruct((nb, RADIX, m, FOURIER_WIDTH), BF16),
            jax.ShapeDtypeStruct((nb, RADIX, m, FOURIER_WIDTH), BF16),
            jax.ShapeDtypeStruct((nb, RADIX, m, CONV_WIDTH), F32),
            jax.ShapeDtypeStruct((nb, RADIX, m, D_MODEL), F32),
        ],
        scratch_shapes=[
            pltpu.VMEM((rows, FOURIER_WIDTH), BF16),
            pltpu.VMEM((rows, FOURIER_WIDTH), BF16),
        ],
        compiler_params=_params(("parallel", "parallel"), 48),
        name="k1_in_proj",
    )(x4, eg, eb, win, cs, ka, kb)


def _k2_body(yr_ref, yi_ref, fc_ref, fs_ref, o_ref, *, group):
    for g in range(group):
        o_ref[0, g] = _dot(fc_ref[g], yr_ref[0, g]) + _dot(fs_ref[g], yi_ref[0, g])


def _k2(yr, yi, fc, fs, group=4):
    nb, _, m, cw = yr.shape
    blk = lambda b, j: (b, j, 0, 0)
    mat = lambda b, j: (j, 0, 0)
    return pl.pallas_call(
        functools.partial(_k2_body, group=group),
        grid=(nb, RADIX // group),
        in_specs=[
            pl.BlockSpec((1, group, m, cw), blk),
            pl.BlockSpec((1, group, m, cw), blk),
            pl.BlockSpec((group, m, m), mat),
            pl.BlockSpec((group, m, m), mat),
        ],
        out_specs=pl.BlockSpec((1, group, m, cw), blk),
        out_shape=jax.ShapeDtypeStruct(yr.shape, F32),
        compiler_params=_params(("parallel", "parallel"), 40),
        name="k2_seq_dft_minor",
    )(yr, yi, fc, fs)


CONV_ROWS = 64


def _k3_body(axn_ref, fm_ref, h_ref, hp_ref, hn_ref, wdw_ref, bdw_ref,
             cg_ref, cb_ref, wout_ref, g1_ref, b1_ref, o_ref, hbuf, cbuf, fmt, heads, *, tm):
    i = pl.program_id(1)
    last = pl.num_programs(1) - 1
    for l in range(CONV_WIDTH // LANES):
        lanes = slice(l * LANES, (l + 1) * LANES)
        hbuf[l, 0:HALO, :] = jnp.where(i > 0, hp_ref[0, :, lanes], 0.0)
        hbuf[l, HALO:HALO + tm, :] = h_ref[0, :, lanes]
        hbuf[l, HALO + tm:, :] = jnp.where(i < last, hn_ref[0, :, lanes], 0.0)

    nl = FOURIER_WIDTH // LANES
    for k1 in range(RADIX):
        for l in range(nl):
            fmt[l, pl.ds(k1, tm // RADIX, stride=RADIX), :] = fm_ref[0, k1, :, l * LANES:(l + 1) * LANES]
    for l in range(nl):
        heads[:, l * LANES:(l + 1) * LANES] = fmt[l].astype(BF16)

    base = HALO - CONV_PAD

    def conv_lane_chunk(l, carry):
        lanes = pl.ds(pl.multiple_of(l * LANES, LANES), LANES)
        for r0 in range(0, tm, CONV_ROWS):
            acc = jnp.zeros((CONV_ROWS, LANES), F32)
            for j in range(CONV_KERNEL):
                acc = acc + hbuf[l, r0 + base + j:r0 + base + j + CONV_ROWS, :] * wdw_ref[j:j + 1, lanes]
            cbuf[r0:r0 + CONV_ROWS, lanes] = acc + bdw_ref[:, lanes]
        return carry

    lax.fori_loop(0, CONV_WIDTH // LANES, conv_lane_chunk, 0)

    cn = _ln(cbuf[...], cg_ref[...], cb_ref[...])
    heads[:, FOURIER_WIDTH:] = (cn * jax.nn.sigmoid(cn)).astype(BF16)
    mix = _dot(heads[...], wout_ref[...])
    o_ref[0] = _ln(axn_ref[0] + mix, g1_ref[...], b1_ref[...])


def _k3(axn, fm4, h, wdw, bdw, cg, cb, wout, g1, b1, tm=512):
    nb, seq, _ = axn.shape
    nh = tm // HALO
    blk = lambda b, i: (b, i, 0)
    fblk = lambda b, i: (b, 0, i, 0)
    prev = lambda b, i: (b, jnp.maximum(i * nh - 1, 0), 0)
    nxt = lambda b, i: (b, jnp.minimum((i + 1) * nh, seq // HALO - 1), 0)
    return pl.pallas_call(
        functools.partial(_k3_body, tm=tm),
        grid=(nb, seq // tm),
        in_specs=[
            pl.BlockSpec((1, tm, D_MODEL), blk),
            pl.BlockSpec((1, RADIX, tm // RADIX, FOURIER_WIDTH), fblk),
            pl.BlockSpec((1, tm, CONV_WIDTH), blk),
            pl.BlockSpec((1, HALO, CONV_WIDTH), prev),
            pl.BlockSpec((1, HALO, CONV_WIDTH), nxt),
            _const_spec(wdw.shape),
            _const_spec((1, CONV_WIDTH)),
            _const_spec((1, CONV_WIDTH)),
            _const_spec((1, CONV_WIDTH)),
            _const_spec(wout.shape),
            _const_spec((1, D_MODEL)),
            _const_spec((1, D_MODEL)),
        ],
        out_specs=pl.BlockSpec((1, tm, D_MODEL), blk),
        out_shape=jax.ShapeDtypeStruct(axn.shape, F32),
        scratch_shapes=[
            pltpu.VMEM((CONV_WIDTH // LANES, tm + 2 * HALO, LANES), F32),
            pltpu.VMEM((tm, CONV_WIDTH), F32),
            pltpu.VMEM((FOURIER_WIDTH // LANES, tm, LANES), F32),
            pltpu.VMEM((tm, D_MODEL), BF16),
        ],
        compiler_params=_params(("parallel", "parallel"), 52),
        name="k3_mix",
    )(axn, fm4, h, h, h, wdw, bdw, cg, cb, wout, g1, b1)


TAIL_ROWS = 256


def _k4_body(x1_ref, w1_ref, bf1_ref, w2_ref, bf2_ref, g2_ref, b2_ref, p_ref, wg_ref, bg_ref,
             wp_ref, g3_ref, b3_ref, o_ref, xb, *, tm):
    f = pl.program_id(1)

    @pl.when(f == 0)
    def _():
        xb[...] = x1_ref[...].astype(BF16)
        o_ref[...] = jnp.zeros_like(o_ref)

    hid = jnp.maximum(_dot(xb[...], w1_ref[...]) + bf1_ref[...], 0.0)
    o_ref[...] += _dot((hid * hid).astype(BF16), w2_ref[...])

    @pl.when(f == pl.num_programs(1) - 1)
    def _():
        for r0 in range(0, tm, TAIL_ROWS):
            rows = slice(r0, r0 + TAIL_ROWS)
            y = ALPHA * x1_ref[rows, :] + (o_ref[rows, :] + bf2_ref[...])
            x2 = _ln(y, g2_ref[...], b2_ref[...])
            gate = jax.nn.sigmoid(_dot(x2.astype(BF16), wg_ref[...]) + bg_ref[...])
            e = gate * _dot(p_ref[rows, :].astype(BF16), wp_ref[...])
            o_ref[rows, :] = _ln(x2 + e, g3_ref[...], b3_ref[...])


def _k4(x1, p2d, w1, bf1, w2, bf2, g2, b2, wg, bg, wp, g3, b3, tm=512, tf=1024):
    t = x1.shape[0]
    return pl.pallas_call(
        functools.partial(_k4_body, tm=tm),
        grid=(t // tm, D_FF // tf),
        in_specs=[
            pl.BlockSpec((tm, D_MODEL), lambda i, f: (i, 0)),
            pl.BlockSpec((D_MODEL, tf), lambda i, f: (0, f)),
            pl.BlockSpec((1, tf), lambda i, f: (0, f)),
            pl.BlockSpec((tf, D_MODEL), lambda i, f: (f, 0)),
            _const_spec((1, D_MODEL)),
            _const_spec((1, D_MODEL)),
            _const_spec((1, D_MODEL)),
            pl.BlockSpec((tm, PLE_DIM), lambda i, f: (i, 0)),
            _const_spec(wg.shape),
            _const_spec((1, D_MODEL)),
            _const_spec(wp.shape),
            _const_spec((1, D_MODEL)),
            _const_spec((1, D_MODEL)),
        ],
        out_specs=pl.BlockSpec((tm, D_MODEL), lambda i, f: (i, 0)),
        out_shape=jax.ShapeDtypeStruct(x1.shape, F32),
        scratch_shapes=[pltpu.VMEM((tm, D_MODEL), BF16)],
        compiler_params=_params(("parallel", "arbitrary"), 56),
        name="k4_ffn_ple",
    )(x1, w1, bf1, w2, bf2, g2, b2, p2d, wg, bg, wp, g3, b3)


def _trunk(x, p, wts, consts):
    nb, seq, _ = x.shape
    t = nb * seq
    m = seq // RADIX
    cs, ka, kb = consts
    fc, fs = _minor_dft_matrices(seq)

    yr, yi, h, axn = _k1(x.reshape(nb, RADIX, m, D_MODEL), wts["emb_g"], wts["emb_b"], wts["w_in"],
                         cs, ka, kb)
    fm4 = _k2(yr, yi, fc, fs)

    x1 = _k3(axn.reshape(nb, seq, D_MODEL), fm4, h.reshape(nb, seq, CONV_WIDTH),
             wts["w_dw"], wts["b_dw"], wts["conv_g"], wts["conv_b"], wts["w_out"],
             wts["ln1_g"], wts["ln1_b"])
    y = _k4(x1.reshape(t, D_MODEL), p.reshape(t, PLE_DIM), wts["w_ff1"], wts["b_ff1"], wts["w_ff2"],
            wts["b_ff2"], wts["ln2_g"], wts["ln2_b"], wts["w_gate"], wts["b_gate"], wts["w_ple"],
            wts["ln3_g"], wts["ln3_b"])
    return y.reshape(nb, seq, D_MODEL)


def kernel(x_prompt, x_sample, p_prompt, p_sample, emb_ln_g, emb_ln_b, w_in, w_dw, b_dw,
           conv_ln_g, conv_ln_b, w_out, ln1_g, ln1_b, w_ff1, b_ff1, w_ff2, b_ff2,
           ln2_g, ln2_b, w_gate, b_gate, w_ple, ln3_g, ln3_b):
    row = lambda v: v.reshape(1, -1).astype(F32)
    wts = {
        "emb_g": row(emb_ln_g), "emb_b": row(emb_ln_b),
        "w_in": w_in[0].astype(BF16),
        "w_dw": w_dw[0].astype(F32), "b_dw": row(b_dw[0]),
        "conv_g": row(conv_ln_g[0]), "conv_b": row(conv_ln_b[0]),
        "w_out": w_out[0].astype(BF16),
        "ln1_g": row(ln1_g[0]), "ln1_b": row(ln1_b[0]),
        "w_ff1": w_ff1[0].astype(BF16), "b_ff1": row(b_ff1[0]),
        "w_ff2": w_ff2[0].astype(BF16), "b_ff2": row(b_ff2[0]),
        "ln2_g": row(ln2_g[0]), "ln2_b": row(ln2_b[0]),
        "w_gate": w_gate[0].astype(BF16), "b_gate": row(b_gate[0]),
        "w_ple": w_ple[0].astype(BF16),
        "ln3_g": row(ln3_g[0]), "ln3_b": row(ln3_b[0]),
    }
    ka, kb = _radix_matrices()
    consts = (_mxu_const(_channel_dft_matrix()), _mxu_const(ka), _mxu_const(kb))
    y_prompt = _trunk(x_prompt, p_prompt[0], wts, consts)
    y_sample = _trunk(x_sample, p_sample[0], wts, consts)
    return (y_prompt, y_sample)
```

```python
import functools

import numpy as np
import jax
import jax.numpy as jnp
from jax import lax
from jax.experimental import pallas as pl
from jax.experimental.pallas import tpu as pltpu

D_MODEL = 2048
FOURIER_WIDTH = 1024
CONV_WIDTH = 1024
FOURIER_GROUPS = 4
FOURIER_GROUP_DIM = 256
CONV_KERNEL = 31
CONV_PAD = 15
D_FF = 8192
PLE_DIM = 256
DEPTH = 1
ALPHA = (2.0 * DEPTH) ** 0.25
LN_EPS = 1e-5

LANES = 128
HALO = 16
RADIX = 16
SLAB = 16
MIB = 1024 * 1024

F32 = jnp.float32
BF16 = jnp.bfloat16


def _ln(x, g, b):
    mu = jnp.mean(x, axis=-1, keepdims=True)
    xc = x - mu
    var = jnp.mean(xc * xc, axis=-1, keepdims=True)
    return xc * lax.rsqrt(var + LN_EPS) * g + b


def _dot(a, b):
    return jnp.dot(a, b, preferred_element_type=F32)


def _params(sem, vmem_mib, flags=None):
    return pltpu.CompilerParams(dimension_semantics=sem, vmem_limit_bytes=vmem_mib * MIB, flags=flags)


def _const_spec(shape):
    nd = len(shape)
    return pl.BlockSpec(shape, lambda *_: (0,) * nd, pipeline_mode=pl.Buffered(1))


def _mxu_const(a):
    return jnp.asarray(np.asarray(a, np.float32)).astype(BF16)


def _cos_sin(n):
    k = np.arange(n, dtype=np.int64)
    ang = 2.0 * np.pi * ((k[:, None] * k[None, :]) % n).astype(np.float64) / n
    return np.cos(ang), np.sin(ang)


def _channel_dft_matrix():
    c, s = _cos_sin(FOURIER_GROUP_DIM)
    scale = 1.0 / np.sqrt(FOURIER_GROUP_DIM)
    return np.concatenate([c, s], axis=1) * scale


def _radix_matrices():
    c, s = _cos_sin(RADIX)
    eye = np.eye(SLAB)
    sc = 1.0 / np.sqrt(RADIX)
    ka = np.concatenate([np.kron(c, eye), np.kron(-s, eye)], axis=0) * sc
    kb = np.concatenate([np.kron(-s, eye), np.kron(-c, eye)], axis=0) * sc
    return ka, kb


def _minor_dft_matrices(seq):
    m = seq // RADIX
    c, s = _cos_sin(m)
    sc = 1.0 / np.sqrt(m)
    fc = jnp.asarray((c * sc).astype(np.float32))[None]
    fs = jnp.asarray((s * sc).astype(np.float32))[None]
    k1 = np.arange(RADIX, dtype=np.int64)[:, None]
    n2 = np.arange(m, dtype=np.int64)[None, :]
    ang = 2.0 * np.pi * ((k1 * n2) % seq).astype(np.float64) / seq
    tc = jnp.asarray(np.cos(ang).astype(np.float32))[:, None, :]
    ts = jnp.asarray(np.sin(ang).astype(np.float32))[:, None, :]
    return (fc * tc - fs * ts).astype(BF16), (fs * tc + fc * ts).astype(BF16)


def _k1_body(*refs, n_cast):
    (x_ref, eg_ref, eb_ref, win_ref, cs_ref, ka_ref, kb_ref), refs = refs[:7], refs[7:]
    cast_in, refs = refs[:n_cast], refs[n_cast:]
    (yr_ref, yi_ref, h_ref, axn_ref), refs = refs[:4], refs[4:]
    cast_out, (a_scr, b_scr) = refs[:n_cast], refs[n_cast:]

    for src, dst in zip(cast_in, cast_out):
        dst[...] = src[...].astype(BF16)

    rows = RADIX * SLAB
    x = x_ref[0].reshape(rows, D_MODEL)
    xn = _ln(x, eg_ref[...], eb_ref[...])
    axn_ref[0] = (ALPHA * xn).reshape(RADIX, SLAB, D_MODEL)
    xn = xn.astype(BF16)

    uv = _dot(xn, win_ref[:, FOURIER_WIDTH:FOURIER_WIDTH + CONV_WIDTH])
    ug = _dot(xn, win_ref[:, FOURIER_WIDTH + CONV_WIDTH:])
    h_ref[0] = (uv * jax.nn.sigmoid(ug)).reshape(RADIX, SLAB, CONV_WIDTH)

    uf = _dot(xn, win_ref[:, 0:FOURIER_WIDTH])
    gd = FOURIER_GROUP_DIM
    for g in range(FOURIER_GROUPS):
        ab = _dot(uf[:, g * gd:(g + 1) * gd].astype(BF16), cs_ref[...])
        a_scr[:, g * gd:(g + 1) * gd] = ab[:, :gd].astype(BF16)
        b_scr[:, g * gd:(g + 1) * gd] = ab[:, gd:].astype(BF16)

    y = _dot(ka_ref[...], a_scr[...]) + _dot(kb_ref[...], b_scr[...])
    yr_ref[0] = y[:rows].astype(BF16).reshape(RADIX, SLAB, FOURIER_WIDTH)
    yi_ref[0] = y[rows:].astype(BF16).reshape(RADIX, SLAB, FOURIER_WIDTH)


def _k1(x4, eg, eb, win, cs, ka, kb, cast=()):
    nb, _, m, _ = x4.shape
    nj = m // SLAB
    blk = lambda b, j: (b, 0, j, 0)
    step = lambda b, j: (b * nj + j, 0)
    rows = RADIX * SLAB
    cast_blocks = [pl.BlockSpec((w.shape[0] // (nb * nj), w.shape[1]), step) for w in cast]
    outs = pl.pallas_call(
        functools.partial(_k1_body, n_cast=len(cast)),
        grid=(nb, nj),
        in_specs=[
            pl.BlockSpec((1, RADIX, SLAB, D_MODEL), blk),
            _const_spec((1, D_MODEL)),
            _const_spec((1, D_MODEL)),
            _const_spec(win.shape),
            _const_spec(cs.shape),
            _const_spec(ka.shape),
            _const_spec(kb.shape),
        ] + cast_blocks,
        out_specs=[
            pl.BlockSpec((1, RADIX, SLAB, FOURIER_WIDTH), blk),
            pl.BlockSpec((1, RADIX, SLAB, FOURIER_WIDTH), blk),
            pl.BlockSpec((1, RADIX, SLAB, CONV_WIDTH), blk),
            pl.BlockSpec((1, RADIX, SLAB, D_MODEL), blk),
        ] + cast_blocks,
        out_shape=[
            jax.ShapeDtypeStruct((nb, RADIX, m, FOURIER_WIDTH), BF16),
            jax.ShapeDtypeStruct((nb, RADIX, m, FOURIER_WIDTH), BF16),
            jax.ShapeDtypeStruct((nb, RADIX, m, CONV_WIDTH), F32),
            jax.ShapeDtypeStruct((nb, RADIX, m, D_MODEL), F32),
        ] + [jax.ShapeDtypeStruct(w.shape, BF16) for w in cast],
        scratch_shapes=[
            pltpu.VMEM((rows, FOURIER_WIDTH), BF16),
            pltpu.VMEM((rows, FOURIER_WIDTH), BF16),
        ],
        compiler_params=_params(("parallel", "parallel"), 48),
        name="k1_in_proj",
    )(x4, eg, eb, win, cs, ka, kb, *cast)
    return outs[:4], outs[4:]


def _k2_body(yr_ref, yi_ref, fc_ref, fs_ref, o_ref, *, group):
    for g in range(group):
        o_ref[0, g] = _dot(fc_ref[g], yr_ref[0, g]) + _dot(fs_ref[g], yi_ref[0, g])


def _k2(yr, yi, fc, fs, group=4):
    nb, _, m, cw = yr.shape
    blk = lambda b, j: (b, j, 0, 0)
    mat = lambda b, j: (j, 0, 0)
    return pl.pallas_call(
        functools.partial(_k2_body, group=group),
        grid=(nb, RADIX // group),
        in_specs=[
            pl.BlockSpec((1, group, m, cw), blk),
            pl.BlockSpec((1, group, m, cw), blk),
            pl.BlockSpec((group, m, m), mat),
            pl.BlockSpec((group, m, m), mat),
        ],
        out_specs=pl.BlockSpec((1, group, m, cw), blk),
        out_shape=jax.ShapeDtypeStruct(yr.shape, F32),
        compiler_params=_params(("parallel", "parallel"), 40),
        name="k2_seq_dft_minor",
    )(yr, yi, fc, fs)


CONV_ROWS = 64


def _k3_body(axn_ref, fm_ref, h_ref, hp_ref, hn_ref, wdw_ref, bdw_ref,
             cg_ref, cb_ref, wout_ref, g1_ref, b1_ref, o_ref, hbuf, cbuf, fmt, heads, *, tm):
    i = pl.program_id(1)
    last = pl.num_programs(1) - 1
    for l in range(CONV_WIDTH // LANES):
        lanes = slice(l * LANES, (l + 1) * LANES)
        hbuf[l, 0:HALO, :] = jnp.where(i > 0, hp_ref[0, :, lanes], 0.0)
        hbuf[l, HALO:HALO + tm, :] = h_ref[0, :, lanes]
        hbuf[l, HALO + tm:, :] = jnp.where(i < last, hn_ref[0, :, lanes], 0.0)

    nl = FOURIER_WIDTH // LANES
    for k1 in range(RADIX):
        for l in range(nl):
            fmt[l, pl.ds(k1, tm // RADIX, stride=RADIX), :] = fm_ref[0, k1, :, l * LANES:(l + 1) * LANES]
    for l in range(nl):
        heads[:, l * LANES:(l + 1) * LANES] = fmt[l].astype(BF16)

    base = HALO - CONV_PAD

    def conv_lane_chunk(l, carry):
        lanes = pl.ds(pl.multiple_of(l * LANES, LANES), LANES)
        for r0 in range(0, tm, CONV_ROWS):
            acc = jnp.zeros((CONV_ROWS, LANES), F32)
            for j in range(CONV_KERNEL):
                acc = acc + hbuf[l, r0 + base + j:r0 + base + j + CONV_ROWS, :] * wdw_ref[j:j + 1, lanes]
            cbuf[r0:r0 + CONV_ROWS, lanes] = acc + bdw_ref[:, lanes]
        return carry

    lax.fori_loop(0, CONV_WIDTH // LANES, conv_lane_chunk, 0)

    cn = _ln(cbuf[...], cg_ref[...], cb_ref[...])
    heads[:, FOURIER_WIDTH:] = (cn * jax.nn.sigmoid(cn)).astype(BF16)
    mix = _dot(heads[...], wout_ref[...])
    o_ref[0] = _ln(axn_ref[0] + mix, g1_ref[...], b1_ref[...])


def _k3(axn, fm4, h, wdw, bdw, cg, cb, wout, g1, b1, tm=512):
    nb, seq, _ = axn.shape
    nh = tm // HALO
    blk = lambda b, i: (b, i, 0)
    fblk = lambda b, i: (b, 0, i, 0)
    prev = lambda b, i: (b, jnp.maximum(i * nh - 1, 0), 0)
    nxt = lambda b, i: (b, jnp.minimum((i + 1) * nh, seq // HALO - 1), 0)
    return pl.pallas_call(
        functools.partial(_k3_body, tm=tm),
        grid=(nb, seq // tm),
        in_specs=[
            pl.BlockSpec((1, tm, D_MODEL), blk),
            pl.BlockSpec((1, RADIX, tm // RADIX, FOURIER_WIDTH), fblk),
            pl.BlockSpec((1, tm, CONV_WIDTH), blk),
            pl.BlockSpec((1, HALO, CONV_WIDTH), prev),
            pl.BlockSpec((1, HALO, CONV_WIDTH), nxt),
            _const_spec(wdw.shape),
            _const_spec((1, CONV_WIDTH)),
            _const_spec((1, CONV_WIDTH)),
            _const_spec((1, CONV_WIDTH)),
            _const_spec(wout.shape),
            _const_spec((1, D_MODEL)),
            _const_spec((1, D_MODEL)),
        ],
        out_specs=pl.BlockSpec((1, tm, D_MODEL), blk),
        out_shape=jax.ShapeDtypeStruct(axn.shape, F32),
        scratch_shapes=[
            pltpu.VMEM((CONV_WIDTH // LANES, tm + 2 * HALO, LANES), F32),
            pltpu.VMEM((tm, CONV_WIDTH), F32),
            pltpu.VMEM((FOURIER_WIDTH // LANES, tm, LANES), F32),
            pltpu.VMEM((tm, D_MODEL), BF16),
        ],
        compiler_params=_params(("parallel", "parallel"), 52),
        name="k3_mix",
    )(axn, fm4, h, h, h, wdw, bdw, cg, cb, wout, g1, b1)


TAIL_ROWS = 256


def _k4_body(x1_ref, w1_ref, bf1_ref, w2_ref, bf2_ref, g2_ref, b2_ref, p_ref, wg_ref, bg_ref,
             wp_ref, g3_ref, b3_ref, o_ref, xb, *, tm):
    f = pl.program_id(1)

    @pl.when(f == 0)
    def _():
        xb[...] = x1_ref[...].astype(BF16)
        o_ref[...] = jnp.zeros_like(o_ref)

    hid = jnp.maximum(_dot(xb[...], w1_ref[...]) + bf1_ref[...], 0.0)
    o_ref[...] += _dot((hid * hid).astype(BF16), w2_ref[...])

    @pl.when(f == pl.num_programs(1) - 1)
    def _():
        for r0 in range(0, tm, TAIL_ROWS):
            rows = slice(r0, r0 + TAIL_ROWS)
            y = ALPHA * x1_ref[rows, :] + (o_ref[rows, :] + bf2_ref[...])
            x2 = _ln(y, g2_ref[...], b2_ref[...])
            gate = jax.nn.sigmoid(_dot(x2.astype(BF16), wg_ref[...]) + bg_ref[...])
            e = gate * _dot(p_ref[rows, :].astype(BF16), wp_ref[...])
            o_ref[rows, :] = _ln(x2 + e, g3_ref[...], b3_ref[...])


def _k4(x1, p2d, w1, bf1, w2, bf2, g2, b2, wg, bg, wp, g3, b3, tm=512, tf=1024):
    t = x1.shape[0]
    return pl.pallas_call(
        functools.partial(_k4_body, tm=tm),
        grid=(t // tm, D_FF // tf),
        in_specs=[
            pl.BlockSpec((tm, D_MODEL), lambda i, f: (i, 0)),
            pl.BlockSpec((D_MODEL, tf), lambda i, f: (0, f)),
            pl.BlockSpec((1, tf), lambda i, f: (0, f)),
            pl.BlockSpec((tf, D_MODEL), lambda i, f: (f, 0)),
            _const_spec((1, D_MODEL)),
            _const_spec((1, D_MODEL)),
            _const_spec((1, D_MODEL)),
            pl.BlockSpec((tm, PLE_DIM), lambda i, f: (i, 0)),
            _const_spec(wg.shape),
            _const_spec((1, D_MODEL)),
            _const_spec(wp.shape),
            _const_spec((1, D_MODEL)),
            _const_spec((1, D_MODEL)),
        ],
        out_specs=pl.BlockSpec((tm, D_MODEL), lambda i, f: (i, 0)),
        out_shape=jax.ShapeDtypeStruct(x1.shape, F32),
        scratch_shapes=[pltpu.VMEM((tm, D_MODEL), BF16)],
        compiler_params=_params(("parallel", "arbitrary"), 56),
        name="k4_ffn_ple",
    )(x1, w1, bf1, w2, bf2, g2, b2, p2d, wg, bg, wp, g3, b3)


CAST_IN_K1 = ("w_out", "w_ff1", "w_ff2", "w_gate")


def _front(x, wts, consts, cast=()):
    nb, seq, _ = x.shape
    m = seq // RADIX
    cs, ka, kb = consts
    fc, fs = _minor_dft_matrices(seq)
    (yr, yi, h, axn), casted = _k1(x.reshape(nb, RADIX, m, D_MODEL), wts["emb_g"], wts["emb_b"],
                                   wts["w_in"], cs, ka, kb, cast)
    fm4 = _k2(yr, yi, fc, fs)
    return (axn.reshape(nb, seq, D_MODEL), fm4, h.reshape(nb, seq, CONV_WIDTH)), casted


def _back(front, p, wts):
    axn, fm4, h = front
    nb, seq, _ = axn.shape
    t = nb * seq
    x1 = _k3(axn, fm4, h, wts["w_dw"], wts["b_dw"], wts["conv_g"], wts["conv_b"], wts["w_out"],
             wts["ln1_g"], wts["ln1_b"])
    y = _k4(x1.reshape(t, D_MODEL), p.reshape(t, PLE_DIM), wts["w_ff1"], wts["b_ff1"], wts["w_ff2"],
            wts["b_ff2"], wts["ln2_g"], wts["ln2_b"], wts["w_gate"], wts["b_gate"], wts["w_ple"],
            wts["ln3_g"], wts["ln3_b"])
    return y.reshape(nb, seq, D_MODEL)


def kernel(x_prompt, x_sample, p_prompt, p_sample, emb_ln_g, emb_ln_b, w_in, w_dw, b_dw,
           conv_ln_g, conv_ln_b, w_out, ln1_g, ln1_b, w_ff1, b_ff1, w_ff2, b_ff2,
           ln2_g, ln2_b, w_gate, b_gate, w_ple, ln3_g, ln3_b):
    row = lambda v: v.reshape(1, -1).astype(F32)
    wts = {
        "emb_g": row(emb_ln_g), "emb_b": row(emb_ln_b),
        "w_in": w_in[0].astype(BF16),
        "w_dw": w_dw[0].astype(F32), "b_dw": row(b_dw[0]),
        "conv_g": row(conv_ln_g[0]), "conv_b": row(conv_ln_b[0]),
        "ln1_g": row(ln1_g[0]), "ln1_b": row(ln1_b[0]),
        "b_ff1": row(b_ff1[0]), "b_ff2": row(b_ff2[0]),
        "ln2_g": row(ln2_g[0]), "ln2_b": row(ln2_b[0]),
        "b_gate": row(b_gate[0]),
        "w_ple": w_ple[0].astype(BF16),
        "ln3_g": row(ln3_g[0]), "ln3_b": row(ln3_b[0]),
    }
    raw = {"w_out": w_out[0], "w_ff1": w_ff1[0], "w_ff2": w_ff2[0], "w_gate": w_gate[0]}
    ka, kb = _radix_matrices()
    consts = (_mxu_const(_channel_dft_matrix()), _mxu_const(ka), _mxu_const(kb))
    front_prompt, casted = _front(x_prompt, wts, consts, tuple(raw[k].astype(F32) for k in CAST_IN_K1))
    wts.update(zip(CAST_IN_K1, casted))
    y_prompt = _back(front_prompt, p_prompt[0], wts)
    front_sample, _ = _front(x_sample, wts, consts)
    y_sample = _back(front_sample, p_sample[0], wts)
    return (y_prompt, y_sample)
```

```python
import functools

import numpy as np
import jax
import jax.numpy as jnp
from jax import lax
from jax.experimental import pallas as pl
from jax.experimental.pallas import tpu as pltpu

D_MODEL = 2048
FOURIER_WIDTH = 1024
CONV_WIDTH = 1024
FOURIER_GROUPS = 4
FOURIER_GROUP_DIM = 256
CONV_KERNEL = 31
CONV_PAD = 15
D_FF = 8192
PLE_DIM = 256
DEPTH = 1
ALPHA = (2.0 * DEPTH) ** 0.25
LN_EPS = 1e-5

LANES = 128
HALO = 16
RADIX = 16
SLAB = 16
MIB = 1024 * 1024

F32 = jnp.float32
BF16 = jnp.bfloat16


def _ln(x, g, b):
    mu = jnp.mean(x, axis=-1, keepdims=True)
    xc = x - mu
    var = jnp.mean(xc * xc, axis=-1, keepdims=True)
    return xc * lax.rsqrt(var + LN_EPS) * g + b


def _dot(a, b):
    return jnp.dot(a, b, preferred_element_type=F32)


def _params(sem, vmem_mib):
    return pltpu.CompilerParams(dimension_semantics=sem, vmem_limit_bytes=vmem_mib * MIB)


def _const_spec(shape):
    nd = len(shape)
    return pl.BlockSpec(shape, lambda *_: (0,) * nd, pipeline_mode=pl.Buffered(1))


def _mxu_const(a):
    return jnp.asarray(np.asarray(a, np.float32)).astype(BF16)


def _cos_sin(n):
    k = np.arange(n, dtype=np.int64)
    ang = 2.0 * np.pi * ((k[:, None] * k[None, :]) % n).astype(np.float64) / n
    return np.cos(ang), np.sin(ang)


def _channel_dft_matrix():
    c, s = _cos_sin(FOURIER_GROUP_DIM)
    scale = 1.0 / np.sqrt(FOURIER_GROUP_DIM)
    return np.concatenate([c, s], axis=1) * scale


def _radix_matrices():
    c, s = _cos_sin(RADIX)
    eye = np.eye(SLAB)
    sc = 1.0 / np.sqrt(RADIX)
    ka = np.concatenate([np.kron(c, eye), np.kron(-s, eye)], axis=0) * sc
    kb = np.concatenate([np.kron(-s, eye), np.kron(-c, eye)], axis=0) * sc
    return ka, kb


def _minor_dft_matrices(seq):
    m = seq // RADIX
    c, s = _cos_sin(m)
    sc = 1.0 / np.sqrt(m)
    fc = jnp.asarray((c * sc).astype(np.float32))[None]
    fs = jnp.asarray((s * sc).astype(np.float32))[None]
    k1 = np.arange(RADIX, dtype=np.int64)[:, None]
    n2 = np.arange(m, dtype=np.int64)[None, :]
    ang = 2.0 * np.pi * ((k1 * n2) % seq).astype(np.float64) / seq
    tc = jnp.asarray(np.cos(ang).astype(np.float32))[:, None, :]
    ts = jnp.asarray(np.sin(ang).astype(np.float32))[:, None, :]
    return (fc * tc - fs * ts).astype(BF16), (fs * tc + fc * ts).astype(BF16)


def _k1_body(*refs, n_cast):
    (x_ref, eg_ref, eb_ref, win_ref, cs_ref, ka_ref, kb_ref), refs = refs[:7], refs[7:]
    cast_in, refs = refs[:n_cast], refs[n_cast:]
    (yr_ref, yi_ref, h_ref, axn_ref), refs = refs[:4], refs[4:]
    cast_out, (a_scr, b_scr) = refs[:n_cast], refs[n_cast:]

    for src, dst in zip(cast_in, cast_out):
        dst[...] = src[...].astype(BF16)

    rows = RADIX * SLAB
    x = x_ref[0].reshape(rows, D_MODEL)
    xn = _ln(x, eg_ref[...], eb_ref[...])
    axn_ref[0] = (ALPHA * xn).reshape(RADIX, SLAB, D_MODEL)
    xn = xn.astype(BF16)

    uv = _dot(xn, win_ref[:, FOURIER_WIDTH:FOURIER_WIDTH + CONV_WIDTH])
    ug = _dot(xn, win_ref[:, FOURIER_WIDTH + CONV_WIDTH:])
    h_ref[0] = (uv * jax.nn.sigmoid(ug)).reshape(RADIX, SLAB, CONV_WIDTH)

    uf = _dot(xn, win_ref[:, 0:FOURIER_WIDTH])
    gd = FOURIER_GROUP_DIM
    for g in range(FOURIER_GROUPS):
        ab = _dot(uf[:, g * gd:(g + 1) * gd].astype(BF16), cs_ref[...])
        a_scr[:, g * gd:(g + 1) * gd] = ab[:, :gd].astype(BF16)
        b_scr[:, g * gd:(g + 1) * gd] = ab[:, gd:].astype(BF16)

    y = _dot(ka_ref[...], a_scr[...]) + _dot(kb_ref[...], b_scr[...])
    yr_ref[0] = y[:rows].astype(BF16).reshape(RADIX, SLAB, FOURIER_WIDTH)
    yi_ref[0] = y[rows:].astype(BF16).reshape(RADIX, SLAB, FOURIER_WIDTH)


def _k1(x4, eg, eb, win, cs, ka, kb, cast=()):
    nb, _, m, _ = x4.shape
    nj = m // SLAB
    blk = lambda b, j: (b, 0, j, 0)
    step = lambda b, j: (b * nj + j, 0)
    rows = RADIX * SLAB
    cast_blocks = [pl.BlockSpec((w.shape[0] // (nb * nj), w.shape[1]), step) for w in cast]
    outs = pl.pallas_call(
        functools.partial(_k1_body, n_cast=len(cast)),
        grid=(nb, nj),
        in_specs=[
            pl.BlockSpec((1, RADIX, SLAB, D_MODEL), blk),
            _const_spec((1, D_MODEL)),
            _const_spec((1, D_MODEL)),
            _const_spec(win.shape),
            _const_spec(cs.shape),
            _const_spec(ka.shape),
            _const_spec(kb.shape),
        ] + cast_blocks,
        out_specs=[
            pl.BlockSpec((1, RADIX, SLAB, FOURIER_WIDTH), blk),
            pl.BlockSpec((1, RADIX, SLAB, FOURIER_WIDTH), blk),
            pl.BlockSpec((1, RADIX, SLAB, CONV_WIDTH), blk),
            pl.BlockSpec((1, RADIX, SLAB, D_MODEL), blk),
        ] + cast_blocks,
        out_shape=[
            jax.ShapeDtypeStruct((nb, RADIX, m, FOURIER_WIDTH), BF16),
            jax.ShapeDtypeStruct((nb, RADIX, m, FOURIER_WIDTH), BF16),
            jax.ShapeDtypeStruct((nb, RADIX, m, CONV_WIDTH), F32),
            jax.ShapeDtypeStruct((nb, RADIX, m, D_MODEL), F32),
        ] + [jax.ShapeDtypeStruct(w.shape, BF16) for w in cast],
        scratch_shapes=[
            pltpu.VMEM((rows, FOURIER_WIDTH), BF16),
            pltpu.VMEM((rows, FOURIER_WIDTH), BF16),
        ],
        compiler_params=_params(("parallel", "parallel"), 48),
        name="k1_in_proj",
    )(x4, eg, eb, win, cs, ka, kb, *cast)
    return outs[:4], outs[4:]


def _k2_body(yr_ref, yi_ref, fc_ref, fs_ref, o_ref, *, group):
    for g in range(group):
        o_ref[0, g] = _dot(fc_ref[g], yr_ref[0, g]) + _dot(fs_ref[g], yi_ref[0, g])


def _k2(yr, yi, fc, fs, group=8):
    nb, _, m, cw = yr.shape
    blk = lambda b, j: (b, j, 0, 0)
    mat = lambda b, j: (j, 0, 0)
    return pl.pallas_call(
        functools.partial(_k2_body, group=group),
        grid=(nb, RADIX // group),
        in_specs=[
            pl.BlockSpec((1, group, m, cw), blk),
            pl.BlockSpec((1, group, m, cw), blk),
            pl.BlockSpec((group, m, m), mat),
            pl.BlockSpec((group, m, m), mat),
        ],
        out_specs=pl.BlockSpec((1, group, m, cw), blk),
        out_shape=jax.ShapeDtypeStruct(yr.shape, F32),
        compiler_params=_params(("parallel", "parallel"), 44),
        name="k2_seq_dft_minor",
    )(yr, yi, fc, fs)


CONV_ROWS = 64


def _k3_body(axn_ref, fm_ref, h_ref, hp_ref, hn_ref, wdw_ref, bdw_ref,
             cg_ref, cb_ref, wout_ref, g1_ref, b1_ref, o_ref, hbuf, cbuf, fmt, heads, *, tm):
    i = pl.program_id(1)
    last = pl.num_programs(1) - 1
    for l in range(CONV_WIDTH // LANES):
        lanes = slice(l * LANES, (l + 1) * LANES)
        hbuf[l, 0:HALO, :] = jnp.where(i > 0, hp_ref[0, :, lanes], 0.0)
        hbuf[l, HALO:HALO + tm, :] = h_ref[0, :, lanes]
        hbuf[l, HALO + tm:, :] = jnp.where(i < last, hn_ref[0, :, lanes], 0.0)

    nl = FOURIER_WIDTH // LANES
    for k1 in range(RADIX):
        for l in range(nl):
            fmt[l, pl.ds(k1, tm // RADIX, stride=RADIX), :] = fm_ref[0, k1, :, l * LANES:(l + 1) * LANES]
    for l in range(nl):
        heads[:, l * LANES:(l + 1) * LANES] = fmt[l].astype(BF16)

    base = HALO - CONV_PAD

    def conv_lane_chunk(l, carry):
        lanes = pl.ds(pl.multiple_of(l * LANES, LANES), LANES)
        for r0 in range(0, tm, CONV_ROWS):
            acc = jnp.zeros((CONV_ROWS, LANES), F32)
            for j in range(CONV_KERNEL):
                acc = acc + hbuf[l, r0 + base + j:r0 + base + j + CONV_ROWS, :] * wdw_ref[j:j + 1, lanes]
            cbuf[r0:r0 + CONV_ROWS, lanes] = acc + bdw_ref[:, lanes]
        return carry

    lax.fori_loop(0, CONV_WIDTH // LANES, conv_lane_chunk, 0)

    cn = _ln(cbuf[...], cg_ref[...], cb_ref[...])
    heads[:, FOURIER_WIDTH:] = (cn * jax.nn.sigmoid(cn)).astype(BF16)
    mix = _dot(heads[...], wout_ref[...])
    o_ref[0] = _ln(axn_ref[0] + mix, g1_ref[...], b1_ref[...])


def _k3(axn, fm4, h, wdw, bdw, cg, cb, wout, g1, b1, tm=512):
    nb, seq, _ = axn.shape
    nh = tm // HALO
    blk = lambda b, i: (b, i, 0)
    fblk = lambda b, i: (b, 0, i, 0)
    prev = lambda b, i: (b, jnp.maximum(i * nh - 1, 0), 0)
    nxt = lambda b, i: (b, jnp.minimum((i + 1) * nh, seq // HALO - 1), 0)
    return pl.pallas_call(
        functools.partial(_k3_body, tm=tm),
        grid=(nb, seq // tm),
        in_specs=[
            pl.BlockSpec((1, tm, D_MODEL), blk),
            pl.BlockSpec((1, RADIX, tm // RADIX, FOURIER_WIDTH), fblk),
            pl.BlockSpec((1, tm, CONV_WIDTH), blk),
            pl.BlockSpec((1, HALO, CONV_WIDTH), prev),
            pl.BlockSpec((1, HALO, CONV_WIDTH), nxt),
            _const_spec(wdw.shape),
            _const_spec((1, CONV_WIDTH)),
            _const_spec((1, CONV_WIDTH)),
            _const_spec((1, CONV_WIDTH)),
            _const_spec(wout.shape),
            _const_spec((1, D_MODEL)),
            _const_spec((1, D_MODEL)),
        ],
        out_specs=pl.BlockSpec((1, tm, D_MODEL), blk),
        out_shape=jax.ShapeDtypeStruct(axn.shape, F32),
        scratch_shapes=[
            pltpu.VMEM((CONV_WIDTH // LANES, tm + 2 * HALO, LANES), F32),
            pltpu.VMEM((tm, CONV_WIDTH), F32),
            pltpu.VMEM((FOURIER_WIDTH // LANES, tm, LANES), F32),
            pltpu.VMEM((tm, D_MODEL), BF16),
        ],
        compiler_params=_params(("parallel", "parallel"), 52),
        name="k3_mix",
    )(axn, fm4, h, h, h, wdw, bdw, cg, cb, wout, g1, b1)


TAIL_ROWS = 256


def _k4_body(x1_ref, w1_ref, bf1_ref, w2_ref, bf2_ref, g2_ref, b2_ref, p_ref, wg_ref, bg_ref,
             wp_ref, g3_ref, b3_ref, o_ref, xb, *, tm):
    f = pl.program_id(1)

    @pl.when(f == 0)
    def _():
        xb[...] = x1_ref[...].astype(BF16)
        o_ref[...] = jnp.zeros_like(o_ref)

    hid = jnp.maximum(_dot(xb[...], w1_ref[...]) + bf1_ref[...], 0.0)
    o_ref[...] += _dot((hid * hid).astype(BF16), w2_ref[...])

    @pl.when(f == pl.num_programs(1) - 1)
    def _():
        for r0 in range(0, tm, TAIL_ROWS):
            rows = slice(r0, r0 + TAIL_ROWS)
            y = ALPHA * x1_ref[rows, :] + (o_ref[rows, :] + bf2_ref[...])
            x2 = _ln(y, g2_ref[...], b2_ref[...])
            gate = jax.nn.sigmoid(_dot(x2.astype(BF16), wg_ref[...]) + bg_ref[...])
            e = gate * _dot(p_ref[rows, :].astype(BF16), wp_ref[...])
            o_ref[rows, :] = _ln(x2 + e, g3_ref[...], b3_ref[...])


def _k4(x1, p2d, w1, bf1, w2, bf2, g2, b2, wg, bg, wp, g3, b3, tm=512, tf=1024):
    t = x1.shape[0]
    return pl.pallas_call(
        functools.partial(_k4_body, tm=tm),
        grid=(t // tm, D_FF // tf),
        in_specs=[
            pl.BlockSpec((tm, D_MODEL), lambda i, f: (i, 0)),
            pl.BlockSpec((D_MODEL, tf), lambda i, f: (0, f)),
            pl.BlockSpec((1, tf), lambda i, f: (0, f)),
            pl.BlockSpec((tf, D_MODEL), lambda i, f: (f, 0)),
            _const_spec((1, D_MODEL)),
            _const_spec((1, D_MODEL)),
            _const_spec((1, D_MODEL)),
            pl.BlockSpec((tm, PLE_DIM), lambda i, f: (i, 0)),
            _const_spec(wg.shape),
            _const_spec((1, D_MODEL)),
            _const_spec(wp.shape),
            _const_spec((1, D_MODEL)),
            _const_spec((1, D_MODEL)),
        ],
        out_specs=pl.BlockSpec((tm, D_MODEL), lambda i, f: (i, 0)),
        out_shape=jax.ShapeDtypeStruct(x1.shape, F32),
        scratch_shapes=[pltpu.VMEM((tm, D_MODEL), BF16)],
        compiler_params=_params(("parallel", "arbitrary"), 56),
        name="k4_ffn_ple",
    )(x1, w1, bf1, w2, bf2, g2, b2, p2d, wg, bg, wp, g3, b3)


CAST_IN_K1 = ("w_out", "w_ff1", "w_ff2", "w_gate")


def _front(x, wts, consts, cast=()):
    nb, seq, _ = x.shape
    m = seq // RADIX
    cs, ka, kb = consts
    fc, fs = _minor_dft_matrices(seq)
    (yr, yi, h, axn), casted = _k1(x.reshape(nb, RADIX, m, D_MODEL), wts["emb_g"], wts["emb_b"],
                                   wts["w_in"], cs, ka, kb, cast)
    fm4 = _k2(yr, yi, fc, fs)
    return (axn.reshape(nb, seq, D_MODEL), fm4, h.reshape(nb, seq, CONV_WIDTH)), casted


def _back(front, p, wts):
    axn, fm4, h = front
    nb, seq, _ = axn.shape
    t = nb * seq
    x1 = _k3(axn, fm4, h, wts["w_dw"], wts["b_dw"], wts["conv_g"], wts["conv_b"], wts["w_out"],
             wts["ln1_g"], wts["ln1_b"])
    y = _k4(x1.reshape(t, D_MODEL), p.reshape(t, PLE_DIM), wts["w_ff1"], wts["b_ff1"], wts["w_ff2"],
            wts["b_ff2"], wts["ln2_g"], wts["ln2_b"], wts["w_gate"], wts["b_gate"], wts["w_ple"],
            wts["ln3_g"], wts["ln3_b"])
    return y.reshape(nb, seq, D_MODEL)


def kernel(x_prompt, x_sample, p_prompt, p_sample, emb_ln_g, emb_ln_b, w_in, w_dw, b_dw,
           conv_ln_g, conv_ln_b, w_out, ln1_g, ln1_b, w_ff1, b_ff1, w_ff2, b_ff2,
           ln2_g, ln2_b, w_gate, b_gate, w_ple, ln3_g, ln3_b):
    row = lambda v: v.reshape(1, -1).astype(F32)
    wts = {
        "emb_g": row(emb_ln_g), "emb_b": row(emb_ln_b),
        "w_in": w_in[0].astype(BF16),
        "w_dw": w_dw[0].astype(F32), "b_dw": row(b_dw[0]),
        "conv_g": row(conv_ln_g[0]), "conv_b": row(conv_ln_b[0]),
        "ln1_g": row(ln1_g[0]), "ln1_b": row(ln1_b[0]),
        "b_ff1": row(b_ff1[0]), "b_ff2": row(b_ff2[0]),
        "ln2_g": row(ln2_g[0]), "ln2_b": row(ln2_b[0]),
        "b_gate": row(b_gate[0]),
        "w_ple": w_ple[0].astype(BF16),
        "ln3_g": row(ln3_g[0]), "ln3_b": row(ln3_b[0]),
    }
    raw = {"w_out": w_out[0], "w_ff1": w_ff1[0], "w_ff2": w_ff2[0], "w_gate": w_gate[0]}
    ka, kb = _radix_matrices()
    consts = (_mxu_const(_channel_dft_matrix()), _mxu_const(ka), _mxu_const(kb))
    front_prompt, casted = _front(x_prompt, wts, consts, tuple(raw[k].astype(F32) for k in CAST_IN_K1))
    wts.update(zip(CAST_IN_K1, casted))
    y_prompt = _back(front_prompt, p_prompt[0], wts)
    front_sample, _ = _front(x_sample, wts, consts)
    y_sample = _back(front_sample, p_sample[0], wts)
    return (y_prompt, y_sample)
```

```python
import functools

import numpy as np
import jax
import jax.numpy as jnp
from jax import lax
from jax.experimental import pallas as pl
from jax.experimental.pallas import tpu as pltpu

D_MODEL = 2048
FOURIER_WIDTH = 1024
CONV_WIDTH = 1024
FOURIER_GROUPS = 4
FOURIER_GROUP_DIM = 256
CONV_KERNEL = 31
CONV_PAD = 15
D_FF = 8192
PLE_DIM = 256
DEPTH = 1
ALPHA = (2.0 * DEPTH) ** 0.25
LN_EPS = 1e-5

LANES = 128
HALO = 16
RADIX = 16
SLAB = 16
MIB = 1024 * 1024

F32 = jnp.float32
BF16 = jnp.bfloat16


def _ln(x, g, b):
    mu = jnp.mean(x, axis=-1, keepdims=True)
    xc = x - mu
    var = jnp.mean(xc * xc, axis=-1, keepdims=True)
    return xc * lax.rsqrt(var + LN_EPS) * g + b


def _dot(a, b):
    return jnp.dot(a, b, preferred_element_type=F32)


def _params(sem, vmem_mib):
    return pltpu.CompilerParams(dimension_semantics=sem, vmem_limit_bytes=vmem_mib * MIB)


def _const_spec(shape):
    nd = len(shape)
    return pl.BlockSpec(shape, lambda *_: (0,) * nd, pipeline_mode=pl.Buffered(1))


def _mxu_const(a):
    return jnp.asarray(np.asarray(a, np.float32)).astype(BF16)


def _cos_sin(n):
    k = np.arange(n, dtype=np.int64)
    ang = 2.0 * np.pi * ((k[:, None] * k[None, :]) % n).astype(np.float64) / n
    return np.cos(ang), np.sin(ang)


def _channel_dft_matrix():
    c, s = _cos_sin(FOURIER_GROUP_DIM)
    scale = 1.0 / np.sqrt(FOURIER_GROUP_DIM)
    return np.concatenate([c, s], axis=1) * scale


def _radix_matrices():
    c, s = _cos_sin(RADIX)
    eye = np.eye(SLAB)
    sc = 1.0 / np.sqrt(RADIX)
    ka = np.concatenate([np.kron(c, eye), np.kron(-s, eye)], axis=0) * sc
    kb = np.concatenate([np.kron(-s, eye), np.kron(-c, eye)], axis=0) * sc
    return ka, kb


def _minor_dft_matrices(seq):
    m = seq // RADIX
    c, s = _cos_sin(m)
    sc = 1.0 / np.sqrt(m)
    fc = jnp.asarray((c * sc).astype(np.float32))[None]
    fs = jnp.asarray((s * sc).astype(np.float32))[None]
    k1 = np.arange(RADIX, dtype=np.int64)[:, None]
    n2 = np.arange(m, dtype=np.int64)[None, :]
    ang = 2.0 * np.pi * ((k1 * n2) % seq).astype(np.float64) / seq
    tc = jnp.asarray(np.cos(ang).astype(np.float32))[:, None, :]
    ts = jnp.asarray(np.sin(ang).astype(np.float32))[:, None, :]
    return (fc * tc - fs * ts).astype(BF16), (fs * tc + fc * ts).astype(BF16)


def _k1_body(*refs, n_cast):
    (x_ref, eg_ref, eb_ref, win_ref, cs_ref, ka_ref, kb_ref), refs = refs[:7], refs[7:]
    cast_in, refs = refs[:n_cast], refs[n_cast:]
    (yr_ref, yi_ref, h_ref, axn_ref), refs = refs[:4], refs[4:]
    cast_out, (a_scr, b_scr) = refs[:n_cast], refs[n_cast:]

    for src, dst in zip(cast_in, cast_out):
        dst[...] = src[...].astype(BF16)

    rows = RADIX * SLAB
    x = x_ref[0].reshape(rows, D_MODEL)
    xn = _ln(x, eg_ref[...], eb_ref[...])
    axn_ref[0] = (ALPHA * xn).reshape(RADIX, SLAB, D_MODEL)
    xn = xn.astype(BF16)

    uv = _dot(xn, win_ref[:, FOURIER_WIDTH:FOURIER_WIDTH + CONV_WIDTH])
    ug = _dot(xn, win_ref[:, FOURIER_WIDTH + CONV_WIDTH:])
    h_ref[0] = (uv * jax.nn.sigmoid(ug)).reshape(RADIX, SLAB, CONV_WIDTH)

    uf = _dot(xn, win_ref[:, 0:FOURIER_WIDTH])
    gd = FOURIER_GROUP_DIM
    for g in range(FOURIER_GROUPS):
        ab = _dot(uf[:, g * gd:(g + 1) * gd].astype(BF16), cs_ref[...])
        a_scr[:, g * gd:(g + 1) * gd] = ab[:, :gd].astype(BF16)
        b_scr[:, g * gd:(g + 1) * gd] = ab[:, gd:].astype(BF16)

    y = _dot(ka_ref[...], a_scr[...]) + _dot(kb_ref[...], b_scr[...])
    yr_ref[0] = y[:rows].astype(BF16).reshape(RADIX, SLAB, FOURIER_WIDTH)
    yi_ref[0] = y[rows:].astype(BF16).reshape(RADIX, SLAB, FOURIER_WIDTH)


def _k1(x4, eg, eb, win, cs, ka, kb, cast=()):
    nb, _, m, _ = x4.shape
    nj = m // SLAB
    blk = lambda b, j: (b, 0, j, 0)
    step = lambda b, j: (b * nj + j, 0)
    rows = RADIX * SLAB
    cast_blocks = [pl.BlockSpec((w.shape[0] // (nb * nj), w.shape[1]), step) for w in cast]
    outs = pl.pallas_call(
        functools.partial(_k1_body, n_cast=len(cast)),
        grid=(nb, nj),
        in_specs=[
            pl.BlockSpec((1, RADIX, SLAB, D_MODEL), blk),
            _const_spec((1, D_MODEL)),
            _const_spec((1, D_MODEL)),
            _const_spec(win.shape),
            _const_spec(cs.shape),
            _const_spec(ka.shape),
            _const_spec(kb.shape),
        ] + cast_blocks,
        out_specs=[
            pl.BlockSpec((1, RADIX, SLAB, FOURIER_WIDTH), blk),
            pl.BlockSpec((1, RADIX, SLAB, FOURIER_WIDTH), blk),
            pl.BlockSpec((1, RADIX, SLAB, CONV_WIDTH), blk),
            pl.BlockSpec((1, RADIX, SLAB, D_MODEL), blk),
        ] + cast_blocks,
        out_shape=[
            jax.ShapeDtypeStruct((nb, RADIX, m, FOURIER_WIDTH), BF16),
            jax.ShapeDtypeStruct((nb, RADIX, m, FOURIER_WIDTH), BF16),
            jax.ShapeDtypeStruct((nb, RADIX, m, CONV_WIDTH), F32),
            jax.ShapeDtypeStruct((nb, RADIX, m, D_MODEL), F32),
        ] + [jax.ShapeDtypeStruct(w.shape, BF16) for w in cast],
        scratch_shapes=[
            pltpu.VMEM((rows, FOURIER_WIDTH), BF16),
            pltpu.VMEM((rows, FOURIER_WIDTH), BF16),
        ],
        compiler_params=_params(("parallel", "parallel"), 48),
        name="k1_in_proj",
    )(x4, eg, eb, win, cs, ka, kb, *cast)
    return outs[:4], outs[4:]


CONV_ROWS = 64
CONV_LN_ROWS = 256


def _k2_body(yr_ref, yi_ref, fc_ref, fs_ref, h_ref, hp_ref, hn_ref, wdw_ref, bdw_ref, cg_ref, cb_ref,
             o_ref, cv_ref, hbuf, cbuf, *, group, rows):
    for g in range(group):
        o_ref[0, g] = _dot(fc_ref[g], yr_ref[0, g]) + _dot(fs_ref[g], yi_ref[0, g])

    j = pl.program_id(1)
    last = pl.num_programs(1) - 1
    for l in range(CONV_WIDTH // LANES):
        lanes = slice(l * LANES, (l + 1) * LANES)
        hbuf[l, 0:HALO, :] = jnp.where(j > 0, hp_ref[0, :, lanes], 0.0)
        hbuf[l, HALO:HALO + rows, :] = h_ref[0, :, lanes]
        hbuf[l, HALO + rows:, :] = jnp.where(j < last, hn_ref[0, :, lanes], 0.0)

    base = HALO - CONV_PAD

    def conv_lane_chunk(l, carry):
        lanes = pl.ds(pl.multiple_of(l * LANES, LANES), LANES)
        for r0 in range(0, rows, CONV_ROWS):
            acc = jnp.zeros((CONV_ROWS, LANES), F32)
            for t in range(CONV_KERNEL):
                acc = acc + hbuf[l, r0 + base + t:r0 + base + t + CONV_ROWS, :] * wdw_ref[t:t + 1, lanes]
            cbuf[r0:r0 + CONV_ROWS, lanes] = acc + bdw_ref[:, lanes]
        return carry

    lax.fori_loop(0, CONV_WIDTH // LANES, conv_lane_chunk, 0)

    for r0 in range(0, rows, CONV_LN_ROWS):
        cn = _ln(cbuf[r0:r0 + CONV_LN_ROWS, :], cg_ref[...], cb_ref[...])
        cv_ref[0, r0:r0 + CONV_LN_ROWS, :] = (cn * jax.nn.sigmoid(cn)).astype(BF16)


def _k2(yr, yi, fc, fs, h, wdw, bdw, cg, cb, group=4):
    nb, _, m, cw = yr.shape
    seq = h.shape[1]
    steps = RADIX // group
    rows = seq // steps
    nh = rows // HALO
    blk = lambda b, j: (b, j, 0, 0)
    mat = lambda b, j: (j, 0, 0)
    seg = lambda b, j: (b, j, 0)
    prev = lambda b, j: (b, jnp.maximum(j * nh - 1, 0), 0)
    nxt = lambda b, j: (b, jnp.minimum((j + 1) * nh, seq // HALO - 1), 0)
    return pl.pallas_call(
        functools.partial(_k2_body, group=group, rows=rows),
        grid=(nb, steps),
        in_specs=[
            pl.BlockSpec((1, group, m, cw), blk),
            pl.BlockSpec((1, group, m, cw), blk),
            pl.BlockSpec((group, m, m), mat),
            pl.BlockSpec((group, m, m), mat),
            pl.BlockSpec((1, rows, CONV_WIDTH), seg),
            pl.BlockSpec((1, HALO, CONV_WIDTH), prev),
            pl.BlockSpec((1, HALO, CONV_WIDTH), nxt),
            _const_spec(wdw.shape),
            _const_spec((1, CONV_WIDTH)),
            _const_spec((1, CONV_WIDTH)),
            _const_spec((1, CONV_WIDTH)),
        ],
        out_specs=[
            pl.BlockSpec((1, group, m, cw), blk),
            pl.BlockSpec((1, rows, CONV_WIDTH), seg),
        ],
        out_shape=[
            jax.ShapeDtypeStruct(yr.shape, F32),
            jax.ShapeDtypeStruct(h.shape, BF16),
        ],
        scratch_shapes=[
            pltpu.VMEM((CONV_WIDTH // LANES, rows + 2 * HALO, LANES), F32),
            pltpu.VMEM((rows, CONV_WIDTH), F32),
        ],
        compiler_params=_params(("parallel", "parallel"), 54),
        name="k2_seq_dft_conv",
    )(yr, yi, fc, fs, h, h, h, wdw, bdw, cg, cb)


def _k3_body(axn_ref, fm_ref, cv_ref, wout_ref, g1_ref, b1_ref, o_ref, fmt, heads, *, tm):
    nl = FOURIER_WIDTH // LANES
    for k1 in range(RADIX):
        for l in range(nl):
            fmt[l, pl.ds(k1, tm // RADIX, stride=RADIX), :] = fm_ref[0, k1, :, l * LANES:(l + 1) * LANES]
    for l in range(nl):
        heads[:, l * LANES:(l + 1) * LANES] = fmt[l].astype(BF16)
    heads[:, FOURIER_WIDTH:] = cv_ref[0]
    mix = _dot(heads[...], wout_ref[...])
    o_ref[0] = _ln(axn_ref[0] + mix, g1_ref[...], b1_ref[...])


def _k3(axn, fm4, cv, wout, g1, b1, tm=512):
    nb, seq, _ = axn.shape
    blk = lambda b, i: (b, i, 0)
    fblk = lambda b, i: (b, 0, i, 0)
    return pl.pallas_call(
        functools.partial(_k3_body, tm=tm),
        grid=(nb, seq // tm),
        in_specs=[
            pl.BlockSpec((1, tm, D_MODEL), blk),
            pl.BlockSpec((1, RADIX, tm // RADIX, FOURIER_WIDTH), fblk),
            pl.BlockSpec((1, tm, CONV_WIDTH), blk),
            _const_spec(wout.shape),
            _const_spec((1, D_MODEL)),
            _const_spec((1, D_MODEL)),
        ],
        out_specs=pl.BlockSpec((1, tm, D_MODEL), blk),
        out_shape=jax.ShapeDtypeStruct(axn.shape, F32),
        scratch_shapes=[
            pltpu.VMEM((FOURIER_WIDTH // LANES, tm, LANES), F32),
            pltpu.VMEM((tm, D_MODEL), BF16),
        ],
        compiler_params=_params(("parallel", "parallel"), 48),
        name="k3_mix",
    )(axn, fm4, cv, wout, g1, b1)


TAIL_ROWS = 256


def _k4_body(x1_ref, w1_ref, bf1_ref, w2_ref, bf2_ref, g2_ref, b2_ref, p_ref, wg_ref, bg_ref,
             wp_ref, g3_ref, b3_ref, o_ref, xb, *, tm):
    f = pl.program_id(1)

    @pl.when(f == 0)
    def _():
        xb[...] = x1_ref[...].astype(BF16)
        o_ref[...] = jnp.zeros_like(o_ref)

    hid = jnp.maximum(_dot(xb[...], w1_ref[...]) + bf1_ref[...], 0.0)
    o_ref[...] += _dot((hid * hid).astype(BF16), w2_ref[...])

    @pl.when(f == pl.num_programs(1) - 1)
    def _():
        for r0 in range(0, tm, TAIL_ROWS):
            rows = slice(r0, r0 + TAIL_ROWS)
            y = ALPHA * x1_ref[rows, :] + (o_ref[rows, :] + bf2_ref[...])
            x2 = _ln(y, g2_ref[...], b2_ref[...])
            gate = jax.nn.sigmoid(_dot(x2.astype(BF16), wg_ref[...]) + bg_ref[...])
            e = gate * _dot(p_ref[rows, :].astype(BF16), wp_ref[...])
            o_ref[rows, :] = _ln(x2 + e, g3_ref[...], b3_ref[...])


def _k4(x1, p2d, w1, bf1, w2, bf2, g2, b2, wg, bg, wp, g3, b3, tm=512, tf=1024):
    t = x1.shape[0]
    return pl.pallas_call(
        functools.partial(_k4_body, tm=tm),
        grid=(t // tm, D_FF // tf),
        in_specs=[
            pl.BlockSpec((tm, D_MODEL), lambda i, f: (i, 0)),
            pl.BlockSpec((D_MODEL, tf), lambda i, f: (0, f)),
            pl.BlockSpec((1, tf), lambda i, f: (0, f)),
            pl.BlockSpec((tf, D_MODEL), lambda i, f: (f, 0)),
            _const_spec((1, D_MODEL)),
            _const_spec((1, D_MODEL)),
            _const_spec((1, D_MODEL)),
            pl.BlockSpec((tm, PLE_DIM), lambda i, f: (i, 0)),
            _const_spec(wg.shape),
            _const_spec((1, D_MODEL)),
            _const_spec(wp.shape),
            _const_spec((1, D_MODEL)),
            _const_spec((1, D_MODEL)),
        ],
        out_specs=pl.BlockSpec((tm, D_MODEL), lambda i, f: (i, 0)),
        out_shape=jax.ShapeDtypeStruct(x1.shape, F32),
        scratch_shapes=[pltpu.VMEM((tm, D_MODEL), BF16)],
        compiler_params=_params(("parallel", "arbitrary"), 56),
        name="k4_ffn_ple",
    )(x1, w1, bf1, w2, bf2, g2, b2, p2d, wg, bg, wp, g3, b3)


CAST_IN_K1 = ("w_out", "w_ff1", "w_ff2", "w_gate")


def _front(x, wts, consts, cast=()):
    nb, seq, _ = x.shape
    m = seq // RADIX
    cs, ka, kb = consts
    fc, fs = _minor_dft_matrices(seq)
    (yr, yi, h, axn), casted = _k1(x.reshape(nb, RADIX, m, D_MODEL), wts["emb_g"], wts["emb_b"],
                                   wts["w_in"], cs, ka, kb, cast)
    fm4, cv = _k2(yr, yi, fc, fs, h.reshape(nb, seq, CONV_WIDTH), wts["w_dw"], wts["b_dw"],
                  wts["conv_g"], wts["conv_b"])
    return (axn.reshape(nb, seq, D_MODEL), fm4, cv), casted


def _back(front, p, wts):
    axn, fm4, cv = front
    nb, seq, _ = axn.shape
    t = nb * seq
    x1 = _k3(axn, fm4, cv, wts["w_out"], wts["ln1_g"], wts["ln1_b"])
    y = _k4(x1.reshape(t, D_MODEL), p.reshape(t, PLE_DIM), wts["w_ff1"], wts["b_ff1"], wts["w_ff2"],
            wts["b_ff2"], wts["ln2_g"], wts["ln2_b"], wts["w_gate"], wts["b_gate"], wts["w_ple"],
            wts["ln3_g"], wts["ln3_b"])
    return y.reshape(nb, seq, D_MODEL)


def kernel(x_prompt, x_sample, p_prompt, p_sample, emb_ln_g, emb_ln_b, w_in, w_dw, b_dw,
           conv_ln_g, conv_ln_b, w_out, ln1_g, ln1_b, w_ff1, b_ff1, w_ff2, b_ff2,
           ln2_g, ln2_b, w_gate, b_gate, w_ple, ln3_g, ln3_b):
    row = lambda v: v.reshape(1, -1).astype(F32)
    wts = {
        "emb_g": row(emb_ln_g), "emb_b": row(emb_ln_b),
        "w_in": w_in[0].astype(BF16),
        "w_dw": w_dw[0].astype(F32), "b_dw": row(b_dw[0]),
        "conv_g": row(conv_ln_g[0]), "conv_b": row(conv_ln_b[0]),
        "ln1_g": row(ln1_g[0]), "ln1_b": row(ln1_b[0]),
        "b_ff1": row(b_ff1[0]), "b_ff2": row(b_ff2[0]),
        "ln2_g": row(ln2_g[0]), "ln2_b": row(ln2_b[0]),
        "b_gate": row(b_gate[0]),
        "w_ple": w_ple[0].astype(BF16),
        "ln3_g": row(ln3_g[0]), "ln3_b": row(ln3_b[0]),
    }
    raw = {"w_out": w_out[0], "w_ff1": w_ff1[0], "w_ff2": w_ff2[0], "w_gate": w_gate[0]}
    ka, kb = _radix_matrices()
    consts = (_mxu_const(_channel_dft_matrix()), _mxu_const(ka), _mxu_const(kb))
    front_prompt, casted = _front(x_prompt, wts, consts, tuple(raw[k].astype(F32) for k in CAST_IN_K1))
    wts.update(zip(CAST_IN_K1, casted))
    y_prompt = _back(front_prompt, p_prompt[0], wts)
    front_sample, _ = _front(x_sample, wts, consts)
    y_sample = _back(front_sample, p_sample[0], wts)
    return (y_prompt, y_sample)
```

```python
import functools

import numpy as np
import jax
import jax.numpy as jnp
from jax import lax
from jax.experimental import pallas as pl
from jax.experimental.pallas import tpu as pltpu

D_MODEL = 2048
FOURIER_WIDTH = 1024
CONV_WIDTH = 1024
FOURIER_GROUPS = 4
FOURIER_GROUP_DIM = 256
CONV_KERNEL = 31
CONV_PAD = 15
D_FF = 8192
PLE_DIM = 256
DEPTH = 1
ALPHA = (2.0 * DEPTH) ** 0.25
LN_EPS = 1e-5

LANES = 128
HALO = 16
RADIX = 16
SLAB = 16
MIB = 1024 * 1024

F32 = jnp.float32
BF16 = jnp.bfloat16


def _ln(x, g, b):
    mu = jnp.mean(x, axis=-1, keepdims=True)
    xc = x - mu
    var = jnp.mean(xc * xc, axis=-1, keepdims=True)
    return xc * lax.rsqrt(var + LN_EPS) * g + b


def _dot(a, b):
    return jnp.dot(a, b, preferred_element_type=F32)


def _params(sem, vmem_mib):
    return pltpu.CompilerParams(dimension_semantics=sem, vmem_limit_bytes=vmem_mib * MIB)


def _const_spec(shape):
    nd = len(shape)
    return pl.BlockSpec(shape, lambda *_: (0,) * nd, pipeline_mode=pl.Buffered(1))


def _mxu_const(a):
    return jnp.asarray(np.asarray(a, np.float32)).astype(BF16)


def _cos_sin(n):
    k = np.arange(n, dtype=np.int64)
    ang = 2.0 * np.pi * ((k[:, None] * k[None, :]) % n).astype(np.float64) / n
    return np.cos(ang), np.sin(ang)


def _channel_dft_matrix():
    c, s = _cos_sin(FOURIER_GROUP_DIM)
    scale = 1.0 / np.sqrt(FOURIER_GROUP_DIM)
    return np.concatenate([c, s], axis=1) * scale


def _radix_matrices():
    c, s = _cos_sin(RADIX)
    eye = np.eye(SLAB)
    sc = 1.0 / np.sqrt(RADIX)
    ka = np.concatenate([np.kron(c, eye), np.kron(-s, eye)], axis=0) * sc
    kb = np.concatenate([np.kron(-s, eye), np.kron(-c, eye)], axis=0) * sc
    return ka, kb


def _minor_dft_matrices(seq):
    m = seq // RADIX
    c, s = _cos_sin(m)
    sc = 1.0 / np.sqrt(m)
    fc = jnp.asarray((c * sc).astype(np.float32))[None]
    fs = jnp.asarray((s * sc).astype(np.float32))[None]
    k1 = np.arange(RADIX, dtype=np.int64)[:, None]
    n2 = np.arange(m, dtype=np.int64)[None, :]
    ang = 2.0 * np.pi * ((k1 * n2) % seq).astype(np.float64) / seq
    tc = jnp.asarray(np.cos(ang).astype(np.float32))[:, None, :]
    ts = jnp.asarray(np.sin(ang).astype(np.float32))[:, None, :]
    return (fc * tc - fs * ts).astype(BF16), (fs * tc + fc * ts).astype(BF16)


def _k1_body(*refs, n_cast):
    (x_ref, eg_ref, eb_ref, win_ref, cs_ref, ka_ref, kb_ref), refs = refs[:7], refs[7:]
    cast_in, refs = refs[:n_cast], refs[n_cast:]
    (yr_ref, yi_ref, h_ref, axn_ref), refs = refs[:4], refs[4:]
    cast_out, (a_scr, b_scr) = refs[:n_cast], refs[n_cast:]

    for src, dst in zip(cast_in, cast_out):
        dst[...] = src[...].astype(BF16)

    rows = RADIX * SLAB
    x = x_ref[0].reshape(rows, D_MODEL)
    xn = _ln(x, eg_ref[...], eb_ref[...])
    axn_ref[0] = (ALPHA * xn).reshape(RADIX, SLAB, D_MODEL)
    xn = xn.astype(BF16)

    uv = _dot(xn, win_ref[:, FOURIER_WIDTH:FOURIER_WIDTH + CONV_WIDTH])
    ug = _dot(xn, win_ref[:, FOURIER_WIDTH + CONV_WIDTH:])
    h_ref[0] = (uv * jax.nn.sigmoid(ug)).reshape(RADIX, SLAB, CONV_WIDTH)

    uf = _dot(xn, win_ref[:, 0:FOURIER_WIDTH])
    gd = FOURIER_GROUP_DIM
    for g in range(FOURIER_GROUPS):
        ab = _dot(uf[:, g * gd:(g + 1) * gd].astype(BF16), cs_ref[...])
        a_scr[:, g * gd:(g + 1) * gd] = ab[:, :gd].astype(BF16)
        b_scr[:, g * gd:(g + 1) * gd] = ab[:, gd:].astype(BF16)

    y = _dot(ka_ref[...], a_scr[...]) + _dot(kb_ref[...], b_scr[...])
    yr_ref[0] = y[:rows].astype(BF16).reshape(RADIX, SLAB, FOURIER_WIDTH)
    yi_ref[0] = y[rows:].astype(BF16).reshape(RADIX, SLAB, FOURIER_WIDTH)


def _k1(x4, eg, eb, win, cs, ka, kb, cast=()):
    nb, _, m, _ = x4.shape
    nj = m // SLAB
    blk = lambda b, j: (b, 0, j, 0)
    step = lambda b, j: (b * nj + j, 0)
    rows = RADIX * SLAB
    cast_blocks = [pl.BlockSpec((w.shape[0] // (nb * nj), w.shape[1]), step) for w in cast]
    outs = pl.pallas_call(
        functools.partial(_k1_body, n_cast=len(cast)),
        grid=(nb, nj),
        in_specs=[
            pl.BlockSpec((1, RADIX, SLAB, D_MODEL), blk),
            _const_spec((1, D_MODEL)),
            _const_spec((1, D_MODEL)),
            _const_spec(win.shape),
            _const_spec(cs.shape),
            _const_spec(ka.shape),
            _const_spec(kb.shape),
        ] + cast_blocks,
        out_specs=[
            pl.BlockSpec((1, RADIX, SLAB, FOURIER_WIDTH), blk),
            pl.BlockSpec((1, RADIX, SLAB, FOURIER_WIDTH), blk),
            pl.BlockSpec((1, RADIX, SLAB, CONV_WIDTH), blk),
            pl.BlockSpec((1, RADIX, SLAB, D_MODEL), blk),
        ] + cast_blocks,
        out_shape=[
            jax.ShapeDtypeStruct((nb, RADIX, m, FOURIER_WIDTH), BF16),
            jax.ShapeDtypeStruct((nb, RADIX, m, FOURIER_WIDTH), BF16),
            jax.ShapeDtypeStruct((nb, RADIX, m, CONV_WIDTH), F32),
            jax.ShapeDtypeStruct((nb, RADIX, m, D_MODEL), F32),
        ] + [jax.ShapeDtypeStruct(w.shape, BF16) for w in cast],
        scratch_shapes=[
            pltpu.VMEM((rows, FOURIER_WIDTH), BF16),
            pltpu.VMEM((rows, FOURIER_WIDTH), BF16),
        ],
        compiler_params=_params(("parallel", "parallel"), 48),
        name="k1_in_proj",
    )(x4, eg, eb, win, cs, ka, kb, *cast)
    return outs[:4], outs[4:]


CONV_ROWS = 64
CONV_LN_ROWS = 256


def _k2_body(yr_ref, yi_ref, fc_ref, fs_ref, h_ref, hp_ref, hn_ref, wdw_ref, bdw_ref, cg_ref, cb_ref,
             o_ref, cv_ref, hbuf, cbuf, *, group, rows):
    for g in range(group):
        o_ref[0, g] = _dot(fc_ref[g], yr_ref[0, g]) + _dot(fs_ref[g], yi_ref[0, g])

    j = pl.program_id(1)
    last = pl.num_programs(1) - 1
    for l in range(CONV_WIDTH // LANES):
        lanes = slice(l * LANES, (l + 1) * LANES)
        hbuf[l, 0:HALO, :] = jnp.where(j > 0, hp_ref[0, :, lanes], 0.0)
        hbuf[l, HALO:HALO + rows, :] = h_ref[0, :, lanes]
        hbuf[l, HALO + rows:, :] = jnp.where(j < last, hn_ref[0, :, lanes], 0.0)

    base = HALO - CONV_PAD

    def conv_lane_chunk(l, carry):
        lanes = pl.ds(pl.multiple_of(l * LANES, LANES), LANES)
        for r0 in range(0, rows, CONV_ROWS):
            acc = hbuf[l, r0 + base:r0 + base + CONV_ROWS, :] * wdw_ref[0:1, lanes] + bdw_ref[:, lanes]
            for t in range(1, CONV_KERNEL):
                acc = acc + hbuf[l, r0 + base + t:r0 + base + t + CONV_ROWS, :] * wdw_ref[t:t + 1, lanes]
            cbuf[r0:r0 + CONV_ROWS, lanes] = acc
        return carry

    lax.fori_loop(0, CONV_WIDTH // LANES, conv_lane_chunk, 0)

    for r0 in range(0, rows, CONV_LN_ROWS):
        cn = _ln(cbuf[r0:r0 + CONV_LN_ROWS, :], cg_ref[...], cb_ref[...])
        cv_ref[0, r0:r0 + CONV_LN_ROWS, :] = (cn * jax.nn.sigmoid(cn)).astype(BF16)


def _k2(yr, yi, fc, fs, h, wdw, bdw, cg, cb, group=4):
    nb, _, m, cw = yr.shape
    seq = h.shape[1]
    steps = RADIX // group
    rows = seq // steps
    nh = rows // HALO
    blk = lambda b, j: (b, j, 0, 0)
    mat = lambda b, j: (j, 0, 0)
    seg = lambda b, j: (b, j, 0)
    prev = lambda b, j: (b, jnp.maximum(j * nh - 1, 0), 0)
    nxt = lambda b, j: (b, jnp.minimum((j + 1) * nh, seq // HALO - 1), 0)
    return pl.pallas_call(
        functools.partial(_k2_body, group=group, rows=rows),
        grid=(nb, steps),
        in_specs=[
            pl.BlockSpec((1, group, m, cw), blk),
            pl.BlockSpec((1, group, m, cw), blk),
            pl.BlockSpec((group, m, m), mat),
            pl.BlockSpec((group, m, m), mat),
            pl.BlockSpec((1, rows, CONV_WIDTH), seg),
            pl.BlockSpec((1, HALO, CONV_WIDTH), prev),
            pl.BlockSpec((1, HALO, CONV_WIDTH), nxt),
            _const_spec(wdw.shape),
            _const_spec((1, CONV_WIDTH)),
            _const_spec((1, CONV_WIDTH)),
            _const_spec((1, CONV_WIDTH)),
        ],
        out_specs=[
            pl.BlockSpec((1, group, m, cw), blk),
            pl.BlockSpec((1, rows, CONV_WIDTH), seg),
        ],
        out_shape=[
            jax.ShapeDtypeStruct(yr.shape, F32),
            jax.ShapeDtypeStruct(h.shape, BF16),
        ],
        scratch_shapes=[
            pltpu.VMEM((CONV_WIDTH // LANES, rows + 2 * HALO, LANES), F32),
            pltpu.VMEM((rows, CONV_WIDTH), F32),
        ],
        compiler_params=_params(("parallel", "parallel"), 54),
        name="k2_seq_dft_conv",
    )(yr, yi, fc, fs, h, h, h, wdw, bdw, cg, cb)


def _k3_body(axn_ref, fm_ref, cv_ref, perm_ref, wout_ref, g1_ref, b1_ref, o_ref, ob_ref, heads, *, tm):
    fm = fm_ref[0].reshape(tm, FOURIER_WIDTH).astype(BF16)
    heads[:, 0:FOURIER_WIDTH] = _dot(perm_ref[...], fm).astype(BF16)
    heads[:, FOURIER_WIDTH:] = cv_ref[0]
    mix = _dot(heads[...], wout_ref[...])
    x1 = _ln(axn_ref[0] + mix, g1_ref[...], b1_ref[...])
    o_ref[0] = x1
    ob_ref[0] = x1.astype(BF16)


def _k3(axn, fm4, cv, wout, g1, b1, tm=512):
    nb, seq, _ = axn.shape
    src = np.arange(tm)
    dst = src // (tm // RADIX) + RADIX * (src % (tm // RADIX))
    perm = np.zeros((tm, tm), np.float32)
    perm[dst, src] = 1.0
    perm = _mxu_const(perm)
    blk = lambda b, i: (b, i, 0)
    fblk = lambda b, i: (b, 0, i, 0)
    return pl.pallas_call(
        functools.partial(_k3_body, tm=tm),
        grid=(nb, seq // tm),
        in_specs=[
            pl.BlockSpec((1, tm, D_MODEL), blk),
            pl.BlockSpec((1, RADIX, tm // RADIX, FOURIER_WIDTH), fblk),
            pl.BlockSpec((1, tm, CONV_WIDTH), blk),
            _const_spec(perm.shape),
            _const_spec(wout.shape),
            _const_spec((1, D_MODEL)),
            _const_spec((1, D_MODEL)),
        ],
        out_specs=[pl.BlockSpec((1, tm, D_MODEL), blk), pl.BlockSpec((1, tm, D_MODEL), blk)],
        out_shape=[jax.ShapeDtypeStruct(axn.shape, F32), jax.ShapeDtypeStruct(axn.shape, BF16)],
        scratch_shapes=[pltpu.VMEM((tm, D_MODEL), BF16)],
        compiler_params=_params(("parallel", "parallel"), 48),
        name="k3_mix",
    )(axn, fm4, cv, perm, wout, g1, b1)


TAIL_ROWS = 256


def _k4_body(x1_ref, xb_ref, w1_ref, bf1_ref, w2_ref, bf2_ref, g2_ref, b2_ref, p_ref, wg_ref, bg_ref,
             wp_ref, g3_ref, b3_ref, o_ref, *, tm):
    f = pl.program_id(1)

    @pl.when(f == 0)
    def _():
        o_ref[...] = jnp.zeros_like(o_ref)

    hid = jnp.maximum(_dot(xb_ref[...], w1_ref[...]) + bf1_ref[...], 0.0)
    o_ref[...] += _dot((hid * hid).astype(BF16), w2_ref[...])

    @pl.when(f == pl.num_programs(1) - 1)
    def _():
        for r0 in range(0, tm, TAIL_ROWS):
            rows = slice(r0, r0 + TAIL_ROWS)
            y = ALPHA * x1_ref[rows, :] + (o_ref[rows, :] + bf2_ref[...])
            x2 = _ln(y, g2_ref[...], b2_ref[...])
            gate = jax.nn.sigmoid(_dot(x2.astype(BF16), wg_ref[...]) + bg_ref[...])
            e = gate * _dot(p_ref[rows, :].astype(BF16), wp_ref[...])
            o_ref[rows, :] = _ln(x2 + e, g3_ref[...], b3_ref[...])


def _k4(x1, x1b, p2d, w1, bf1, w2, bf2, g2, b2, wg, bg, wp, g3, b3, tm=512, tf=1024):
    t = x1.shape[0]
    return pl.pallas_call(
        functools.partial(_k4_body, tm=tm),
        grid=(t // tm, D_FF // tf),
        in_specs=[
            pl.BlockSpec((tm, D_MODEL), lambda i, f: (i, 0)),
            pl.BlockSpec((tm, D_MODEL), lambda i, f: (i, 0)),
            pl.BlockSpec((D_MODEL, tf), lambda i, f: (0, f)),
            pl.BlockSpec((1, tf), lambda i, f: (0, f)),
            pl.BlockSpec((tf, D_MODEL), lambda i, f: (f, 0)),
            _const_spec((1, D_MODEL)),
            _const_spec((1, D_MODEL)),
            _const_spec((1, D_MODEL)),
            pl.BlockSpec((tm, PLE_DIM), lambda i, f: (i, 0)),
            _const_spec(wg.shape),
            _const_spec((1, D_MODEL)),
            _const_spec(wp.shape),
            _const_spec((1, D_MODEL)),
            _const_spec((1, D_MODEL)),
        ],
        out_specs=pl.BlockSpec((tm, D_MODEL), lambda i, f: (i, 0)),
        out_shape=jax.ShapeDtypeStruct(x1.shape, F32),
        compiler_params=_params(("parallel", "arbitrary"), 56),
        name="k4_ffn_ple",
    )(x1, x1b, w1, bf1, w2, bf2, g2, b2, p2d, wg, bg, wp, g3, b3)


CAST_IN_K1 = ("w_out", "w_ff1", "w_ff2", "w_gate")


def _front(x, wts, consts, cast=()):
    nb, seq, _ = x.shape
    m = seq // RADIX
    cs, ka, kb = consts
    fc, fs = _minor_dft_matrices(seq)
    (yr, yi, h, axn), casted = _k1(x.reshape(nb, RADIX, m, D_MODEL), wts["emb_g"], wts["emb_b"],
                                   wts["w_in"], cs, ka, kb, cast)
    fm4, cv = _k2(yr, yi, fc, fs, h.reshape(nb, seq, CONV_WIDTH), wts["w_dw"], wts["b_dw"],
                  wts["conv_g"], wts["conv_b"])
    return (axn.reshape(nb, seq, D_MODEL), fm4, cv), casted


def _back(front, p, wts):
    axn, fm4, cv = front
    nb, seq, _ = axn.shape
    t = nb * seq
    x1, x1b = _k3(axn, fm4, cv, wts["w_out"], wts["ln1_g"], wts["ln1_b"])
    y = _k4(x1.reshape(t, D_MODEL), x1b.reshape(t, D_MODEL), p.reshape(t, PLE_DIM), wts["w_ff1"], wts["b_ff1"], wts["w_ff2"],
            wts["b_ff2"], wts["ln2_g"], wts["ln2_b"], wts["w_gate"], wts["b_gate"], wts["w_ple"],
            wts["ln3_g"], wts["ln3_b"])
    return y.reshape(nb, seq, D_MODEL)


def kernel(x_prompt, x_sample, p_prompt, p_sample, emb_ln_g, emb_ln_b, w_in, w_dw, b_dw,
           conv_ln_g, conv_ln_b, w_out, ln1_g, ln1_b, w_ff1, b_ff1, w_ff2, b_ff2,
           ln2_g, ln2_b, w_gate, b_gate, w_ple, ln3_g, ln3_b):
    row = lambda v: v.reshape(1, -1).astype(F32)
    wts = {
        "emb_g": row(emb_ln_g), "emb_b": row(emb_ln_b),
        "w_in": w_in[0].astype(BF16),
        "w_dw": w_dw[0].astype(F32), "b_dw": row(b_dw[0]),
        "conv_g": row(conv_ln_g[0]), "conv_b": row(conv_ln_b[0]),
        "ln1_g": row(ln1_g[0]), "ln1_b": row(ln1_b[0]),
        "b_ff1": row(b_ff1[0]), "b_ff2": row(b_ff2[0]),
        "ln2_g": row(ln2_g[0]), "ln2_b": row(ln2_b[0]),
        "b_gate": row(b_gate[0]),
        "w_ple": w_ple[0].astype(BF16),
        "ln3_g": row(ln3_g[0]), "ln3_b": row(ln3_b[0]),
    }
    raw = {"w_out": w_out[0], "w_ff1": w_ff1[0], "w_ff2": w_ff2[0], "w_gate": w_gate[0]}
    ka, kb = _radix_matrices()
    consts = (_mxu_const(_channel_dft_matrix()), _mxu_const(ka), _mxu_const(kb))
    front_prompt, casted = _front(x_prompt, wts, consts, tuple(raw[k].astype(F32) for k in CAST_IN_K1))
    wts.update(zip(CAST_IN_K1, casted))
    y_prompt = _back(front_prompt, p_prompt[0], wts)
    front_sample, _ = _front(x_sample, wts, consts)
    y_sample = _back(front_sample, p_sample[0], wts)
    return (y_prompt, y_sample)
```
